```python
import math
import jax
import jax.numpy as jnp
from jax import lax
import numpy as np


D_MODEL = 2048
BATCH = 8
SEQ = 4096
DEPTH = 2
DEC_BATCH = 2
DEC_SEQ = 4096
PAST_LEN = 128

HEAD_DIM = 128
ATTN_WIDTH = D_MODEL // 2
N_Q_HEADS = ATTN_WIDTH // HEAD_DIM
N_KV_HEADS = 2
KV_WIDTH = N_KV_HEADS * HEAD_DIM
HYENA_WIDTH = D_MODEL // 4
POOL_WIDTH = D_MODEL - ATTN_WIDTH - HYENA_WIDTH
IN_WIDTH = ATTN_WIDTH + 2 * KV_WIDTH + 3 * HYENA_WIDTH + POOL_WIDTH

WINDOW = 128
BLOCK = 128
ROPE_THETA = 500000.0
ROPE_DIM = HEAD_DIM // 4
NEG_BIG = -1e30

SHORT_CONV = 3
FILTER_EMB = 33
FILTER_BANDS = (FILTER_EMB - 1) // 2
FILTER_ORDER = 64
DECAY_TARGET = 1e-2
SHORT_DECAY_PCT = 0.3
LONG_DECAY_PCT = 1.5

POOL_WINDOWS = (2, 4, 8, 16)
POOL_GROUP = POOL_WIDTH // len(POOL_WINDOWS)

OUT_GROUP = 128
N_OUT_GROUPS = D_MODEL // OUT_GROUP

N_EXPERTS = 64
TOP_K = 8
N_EXPERT_GROUPS = 8
TOPK_GROUPS = 4
EXPERT_HIDDEN = 512
SHARED_HIDDEN = 512
ROUTED_SCALE = 2.5
MOE_BLOCK = 128

ALPHA = (2 * DEPTH) ** 0.25
BETA = (8 * DEPTH) ** -0.25
LN_EPS = 1e-5
RMS_EPS = 1e-6

kernel_name = 'hybrid_bidir_encoder_swa_hyena_pool_moe'


def layer_norm(x, g, b):
    xf = x.astype(jnp.float32)
    mu = jnp.mean(xf, axis=-1, keepdims=True)
    var = jnp.mean(jnp.square(xf - mu), axis=-1, keepdims=True)
    return ((xf - mu) * lax.rsqrt(var + LN_EPS) * g.astype(jnp.float32) + b.astype(jnp.float32)).astype(x.dtype)


def head_rms_norm(y, g):
    B, L, _ = y.shape
    yf = y.astype(jnp.float32).reshape(B, L, N_OUT_GROUPS, OUT_GROUP)
    yf = yf * lax.rsqrt(jnp.mean(jnp.square(yf), axis=-1, keepdims=True) + RMS_EPS)
    return (yf.reshape(B, L, D_MODEL) * g.astype(jnp.float32)).astype(y.dtype)


def partial_rotary(x, pos):
    half = ROPE_DIM // 2
    inv = ROPE_THETA ** (-jnp.arange(half, dtype=jnp.float32) / half)
    ang = pos.astype(jnp.float32)[:, None] * inv[None, :]
    cos = jnp.cos(ang)[None, :, None, :]
    sin = jnp.sin(ang)[None, :, None, :]
    xr = x[..., :ROPE_DIM].astype(jnp.float32)
    x1, x2 = xr[..., :half], xr[..., half:]
    rot = jnp.concatenate([x1 * cos - x2 * sin, x2 * cos + x1 * sin], axis=-1)
    return jnp.concatenate([rot.astype(x.dtype), x[..., ROPE_DIM:]], axis=-1)


def windowed_sink_attention(q, k, v, sink):
    B, L, _, _ = q.shape
    nb = L // BLOCK
    grp = N_Q_HEADS // N_KV_HEADS
    qb = q.reshape(B, nb, BLOCK, N_KV_HEADS, grp, HEAD_DIM)

    def band(t):
        tp = jnp.pad(t, ((0, 0), (BLOCK, BLOCK), (0, 0), (0, 0)))
        tb = tp.reshape(B, nb + 2, BLOCK, N_KV_HEADS, HEAD_DIM)
        return jnp.concatenate([tb[:, :-2], tb[:, 1:-1], tb[:, 2:]], axis=2)

    kb, vb = band(k), band(v)
    s = jnp.einsum('bnqhgd,bnkhd->bnhgqk', qb, kb,
                   preferred_element_type=jnp.float32) * (HEAD_DIM ** -0.5)
    qi = jnp.arange(nb)[:, None] * BLOCK + jnp.arange(BLOCK)[None, :]
    kj = (jnp.arange(nb)[:, None] - 1) * BLOCK + jnp.arange(3 * BLOCK)[None, :]
    valid = (jnp.abs(qi[:, :, None] - kj[:, None, :]) <= WINDOW) \
        & (kj >= 0)[:, None, :] & (kj < L)[:, None, :]
    s = jnp.where(valid[None, :, None, None], s, NEG_BIG)
    sink_b = sink.astype(jnp.float32).reshape(1, 1, N_KV_HEADS, grp, 1, 1)
    m = jnp.maximum(jnp.max(s, axis=-1, keepdims=True), sink_b)
    p = jnp.exp(s - m)
    denom = jnp.sum(p, axis=-1, keepdims=True) + jnp.exp(sink_b - m)
    o = jnp.einsum('bnhgqk,bnkhd->bnqhgd', p / denom, vb.astype(jnp.float32))
    return o.reshape(B, L, ATTN_WIDTH).astype(q.dtype)


def short_conv_centred(u, w, b):
    L = u.shape[1]
    up = jnp.pad(u, ((0, 0), (1, 1), (0, 0)))
    return up[:, :L] * w[0] + up[:, 1:L + 1] * w[1] + up[:, 2:] * w[2] + b


def hyena_filter(L, w1, b1, freq, w2, b2, w3):
    f32 = jnp.float32
    t = jnp.linspace(0.0, 1.0, L, dtype=f32)[:, None]
    w = 2.0 * math.pi * jnp.arange(L, dtype=f32) / L
    bands = jnp.linspace(1e-4, FILTER_BANDS - 1, FILTER_BANDS, dtype=f32)
    ang = w[:, None] * bands[None, :]
    z = jnp.concatenate([t, jnp.cos(ang), -jnp.sin(ang)], axis=-1)
    fr = freq.astype(f32)
    h = jnp.sin(fr * (z @ w1.astype(f32) + b1.astype(f32)))
    h = jnp.sin(fr * (h @ w2.astype(f32) + b2.astype(f32)))
    h = h @ w3.astype(f32)
    max_decay = math.log(DECAY_TARGET) / SHORT_DECAY_PCT
    min_decay = math.log(DECAY_TARGET) / LONG_DECAY_PCT
    deltas = jnp.linspace(min_decay, max_decay, HYENA_WIDTH, dtype=f32)
    decay = jnp.exp(-t * jnp.abs(deltas)[None, :])
    h = h.reshape(L, 2, HYENA_WIDTH) * decay[:, None, :]
    h_fwd, h_bwd = h[:, 0], h[:, 1]
    return jnp.concatenate([h_fwd, jnp.zeros((1, HYENA_WIDTH), f32), h_bwd[1:][::-1]], axis=0)


def bidirectional_long_conv(u, filt, skip):
    L = u.shape[1]
    uf = u.astype(jnp.float32)
    spec = jnp.fft.rfft(uf, n=2 * L, axis=1) * jnp.fft.rfft(filt, n=2 * L, axis=0)[None]
    y = jnp.fft.irfft(spec, n=2 * L, axis=1)[:, :L]
    return (y + uf * skip.astype(jnp.float32)).astype(u.dtype)


def hyena_mixer(u, conv_w, conv_b, f_w1, f_b1, f_freq, f_w2, f_b2, f_w3, skip):
    L = u.shape[1]
    u = short_conv_centred(u, conv_w, conv_b)
    x0, x1, v = jnp.split(u, 3, axis=-1)
    filt = hyena_filter(L, f_w1, f_b1, f_freq, f_w2, f_b2, f_w3)
    return bidirectional_long_conv(v * x1, filt, skip) * x0


def multiscale_pool(u, pool_w, pool_scale):
    B, L, _ = u.shape
    uf = u.astype(jnp.float32)
    csum = jnp.pad(jnp.cumsum(uf, axis=1), ((0, 0), (1, 0), (0, 0)))
    pos = jnp.arange(L)
    diffs = []
    for gi, win in enumerate(POOL_WINDOWS):
        lo = jnp.clip(pos - win // 2, 0, L)
        hi = jnp.clip(pos - win // 2 + win, 0, L)
        sl = slice(gi * POOL_GROUP, (gi + 1) * POOL_GROUP)
        cg = csum[..., sl]
        mean = (cg[:, hi] - cg[:, lo]) / (hi - lo).astype(jnp.float32)[None, :, None]
        diffs.append(mean - uf[..., sl])
    d = jnp.stack(diffs, axis=2)
    y = jnp.einsum('blgc,gcd->blgd', d, pool_w.astype(jnp.float32)).reshape(B, L, POOL_WIDTH)
    return (y * pool_scale.astype(jnp.float32)).astype(u.dtype)


def token_mixer(x, w_in, attn_sink, hy_conv_w, hy_conv_b, hy_f_w1, hy_f_b1, hy_f_freq,
                hy_f_w2, hy_f_b2, hy_f_w3, hy_skip, pool_w, pool_scale, out_norm_g, w_out):
    B, L, _ = x.shape
    proj = x @ w_in
    o1 = ATTN_WIDTH
    o2 = o1 + KV_WIDTH
    o3 = o2 + KV_WIDTH
    o4 = o3 + 3 * HYENA_WIDTH
    pos = jnp.arange(L)
    q = partial_rotary(proj[..., :o1].reshape(B, L, N_Q_HEADS, HEAD_DIM), pos)
    k = partial_rotary(proj[..., o1:o2].reshape(B, L, N_KV_HEADS, HEAD_DIM), pos)
    v = proj[..., o2:o3].reshape(B, L, N_KV_HEADS, HEAD_DIM)
    y_attn = windowed_sink_attention(q, k, v, attn_sink)
    y_hyena = hyena_mixer(proj[..., o3:o4], hy_conv_w, hy_conv_b, hy_f_w1, hy_f_b1, hy_f_freq,
                          hy_f_w2, hy_f_b2, hy_f_w3, hy_skip)
    y_pool = multiscale_pool(proj[..., o4:], pool_w, pool_scale)
    y = jnp.concatenate([y_attn.astype(x.dtype), y_hyena.astype(x.dtype), y_pool.astype(x.dtype)], axis=-1)
    return head_rms_norm(y, out_norm_g) @ w_out


def swiglu(x, wg, wu, wd):
    return (jax.nn.silu(x @ wg) * (x @ wu)) @ wd


def routed_experts(xt, eidx, gates, w_gate, w_up, w_down):
    T, D = xt.shape
    A = T * TOP_K
    n_blocks = -(-A // MOE_BLOCK) + N_EXPERTS
    e_flat = eidx.reshape(A)
    tok_flat = jnp.repeat(jnp.arange(T, dtype=jnp.int32), TOP_K)
    g_flat = gates.reshape(A)
    order = jnp.argsort(e_flat)
    e_sorted = e_flat[order]
    counts = jnp.bincount(e_flat, length=N_EXPERTS)
    starts = jnp.cumsum(counts) - counts
    padded = (counts + MOE_BLOCK - 1) // MOE_BLOCK * MOE_BLOCK
    pstarts = jnp.cumsum(padded) - padded
    slot = pstarts[e_sorted] + jnp.arange(A) - starts[e_sorted]
    slot_tok = jnp.full((n_blocks * MOE_BLOCK,), T, jnp.int32).at[slot].set(tok_flat[order])
    slot_gate = jnp.zeros((n_blocks * MOE_BLOCK,), jnp.float32).at[slot].set(g_flat[order])
    block_end = jnp.cumsum(padded) // MOE_BLOCK
    block_expert = jnp.minimum(jnp.searchsorted(block_end, jnp.arange(n_blocks), side='right'),
                               N_EXPERTS - 1)
    x_pad = jnp.concatenate([xt, jnp.zeros((1, D), xt.dtype)], axis=0)

    def body(acc, blk):
        tok, gate, e = blk
        xb = x_pad[tok]
        yb = swiglu(xb, w_gate[e], w_up[e], w_down[e]).astype(jnp.float32) * gate[:, None]
        return acc.at[tok].add(yb), None

    acc0 = jnp.zeros((T + 1, D), jnp.float32)
    acc, _ = lax.scan(body, acc0, (slot_tok.reshape(n_blocks, MOE_BLOCK),
                                   slot_gate.reshape(n_blocks, MOE_BLOCK), block_expert))
    return acc[:T]


def moe_ffn(x, router_w, router_bias, exp_w_gate, exp_w_up, exp_w_down, sh_w_gate, sh_w_up, sh_w_down):
    B, L, D = x.shape
    T = B * L
    xt = x.reshape(T, D)
    scores = jax.nn.sigmoid((xt @ router_w).astype(jnp.float32))
    choice = scores + router_bias.astype(jnp.float32)[None, :]
    per_grp = N_EXPERTS // N_EXPERT_GROUPS
    grp_score = jnp.sum(lax.top_k(choice.reshape(T, N_EXPERT_GROUPS, per_grp), 2)[0], axis=-1)
    _, gidx = lax.top_k(grp_score, TOPK_GROUPS)
    gmask = jnp.sum(jax.nn.one_hot(gidx, N_EXPERT_GROUPS, dtype=jnp.float32), axis=1) > 0
    masked = jnp.where(jnp.repeat(gmask, per_grp, axis=1), choice, NEG_BIG)
    _, eidx = lax.top_k(masked, TOP_K)
    g = jnp.take_along_axis(scores, eidx, axis=1)
    g = g / jnp.sum(g, axis=-1, keepdims=True) * ROUTED_SCALE
    routed = routed_experts(xt, eidx, g, exp_w_gate, exp_w_up, exp_w_down)
    shared = swiglu(xt, sh_w_gate, sh_w_up, sh_w_down).astype(jnp.float32)
    return (routed + shared).astype(x.dtype).reshape(B, L, D)


def _trunk(x, ln_in_g, ln_in_b, w_in, attn_sink, hy_conv_w, hy_conv_b, hy_f_w1, hy_f_b1,
           hy_f_freq, hy_f_w2, hy_f_b2, hy_f_w3, hy_skip, pool_w, pool_scale, out_norm_g, w_out,
           ln1_g, ln1_b, router_w, router_bias, exp_w_gate, exp_w_up, exp_w_down,
           sh_w_gate, sh_w_up, sh_w_down, ln2_g, ln2_b):
    x = layer_norm(x, ln_in_g, ln_in_b)
    for l in range(DEPTH):
        mix = token_mixer(x, w_in[l], attn_sink[l], hy_conv_w[l], hy_conv_b[l], hy_f_w1[l],
                          hy_f_b1[l], hy_f_freq[l], hy_f_w2[l], hy_f_b2[l], hy_f_w3[l], hy_skip[l],
                          pool_w[l], pool_scale[l], out_norm_g[l], w_out[l])
        x = layer_norm(ALPHA * x + mix, ln1_g[l], ln1_b[l])
        ffn = moe_ffn(x, router_w[l], router_bias[l], exp_w_gate[l], exp_w_up[l], exp_w_down[l],
                      sh_w_gate[l], sh_w_up[l], sh_w_down[l])
        x = layer_norm(ALPHA * x + ffn, ln2_g[l], ln2_b[l])
    return x


def setup_inputs(seed: int = 0) -> dict:
    key = jax.random.key(seed)
    ks = iter(jax.random.split(key, 40))

    def nrm(shape, scale):
        return jax.random.normal(next(ks), shape, jnp.float32) * scale

    D = D_MODEL
    return {
        'x_prompt': nrm((BATCH, SEQ, D), 1.0),
        'x_sample': nrm((DEC_BATCH, DEC_SEQ, D), 1.0),
        'ln_in_g': 1.0 + nrm((D,), 0.02),
        'ln_in_b': nrm((D,), 0.02),
        'w_in': nrm((DEPTH, D, IN_WIDTH), D ** -0.5),
        'attn_sink': nrm((DEPTH, N_Q_HEADS), 0.5),
        'hy_conv_w': nrm((DEPTH, SHORT_CONV, 3 * HYENA_WIDTH), SHORT_CONV ** -0.5),
        'hy_conv_b': nrm((DEPTH, 3 * HYENA_WIDTH), 0.02),
        'hy_f_w1': nrm((DEPTH, FILTER_EMB, FILTER_ORDER), FILTER_EMB ** -0.5),
        'hy_f_b1': nrm((DEPTH, FILTER_ORDER), 0.02),
        'hy_f_freq': 1.0 + nrm((DEPTH, FILTER_ORDER), 0.1),
        'hy_f_w2': nrm((DEPTH, FILTER_ORDER, FILTER_ORDER), FILTER_ORDER ** -0.5),
        'hy_f_b2': nrm((DEPTH, FILTER_ORDER), 0.02),
        'hy_f_w3': nrm((DEPTH, FILTER_ORDER, 2 * HYENA_WIDTH), FILTER_ORDER ** -0.5),
        'hy_skip': nrm((DEPTH, HYENA_WIDTH), 1.0),
        'pool_w': nrm((DEPTH, len(POOL_WINDOWS), POOL_GROUP, POOL_GROUP), POOL_GROUP ** -0.5),
        'pool_scale': 1.0 + nrm((DEPTH, POOL_WIDTH), 0.1),
        'out_norm_g': 1.0 + nrm((DEPTH, D), 0.02),
        'w_out': nrm((DEPTH, D, D), D ** -0.5 * BETA),
        'ln1_g': 1.0 + nrm((DEPTH, D), 0.02),
        'ln1_b': nrm((DEPTH, D), 0.02),
        'router_w': nrm((DEPTH, D, N_EXPERTS), D ** -0.5),
        'router_bias': nrm((DEPTH, N_EXPERTS), 0.01),
        'exp_w_gate': nrm((DEPTH, N_EXPERTS, D, EXPERT_HIDDEN), D ** -0.5),
        'exp_w_up': nrm((DEPTH, N_EXPERTS, D, EXPERT_HIDDEN), D ** -0.5),
        'exp_w_down': nrm((DEPTH, N_EXPERTS, EXPERT_HIDDEN, D), EXPERT_HIDDEN ** -0.5 * BETA),
        'sh_w_gate': nrm((DEPTH, D, SHARED_HIDDEN), D ** -0.5),
        'sh_w_up': nrm((DEPTH, D, SHARED_HIDDEN), D ** -0.5),
        'sh_w_down': nrm((DEPTH, SHARED_HIDDEN, D), SHARED_HIDDEN ** -0.5 * BETA),
        'ln2_g': 1.0 + nrm((DEPTH, D), 0.02),
        'ln2_b': nrm((DEPTH, D), 0.02),
    }


def reference(x_prompt, x_sample, ln_in_g, ln_in_b, w_in, attn_sink, hy_conv_w, hy_conv_b,
              hy_f_w1, hy_f_b1, hy_f_freq, hy_f_w2, hy_f_b2, hy_f_w3, hy_skip, pool_w, pool_scale,
              out_norm_g, w_out, ln1_g, ln1_b, router_w, router_bias, exp_w_gate, exp_w_up,
              exp_w_down, sh_w_gate, sh_w_up, sh_w_down, ln2_g, ln2_b):
    weights = (ln_in_g, ln_in_b, w_in, attn_sink, hy_conv_w, hy_conv_b, hy_f_w1, hy_f_b1,
               hy_f_freq, hy_f_w2, hy_f_b2, hy_f_w3, hy_skip, pool_w, pool_scale, out_norm_g,
               w_out, ln1_g, ln1_b, router_w, router_bias, exp_w_gate, exp_w_up, exp_w_down,
               sh_w_gate, sh_w_up, sh_w_down, ln2_g, ln2_b)
    y_prompt = _trunk(x_prompt, *weights)
    y_sample = _trunk(x_sample, *weights)
    return (y_prompt, y_sample)
```

```python
import functools
import math

import jax
import jax.numpy as jnp
import numpy as np
from jax import lax
from jax.experimental import pallas as pl
from jax.experimental.pallas import tpu as pltpu

F32 = jnp.float32
BF16 = jnp.bfloat16
HIGHEST = lax.Precision.HIGHEST

D_MODEL = 2048
DEPTH = 2
HEAD_DIM = 128
ATTN_WIDTH = D_MODEL // 2
N_Q_HEADS = ATTN_WIDTH // HEAD_DIM
N_KV_HEADS = 2
Q_PER_KV = N_Q_HEADS // N_KV_HEADS
KV_WIDTH = N_KV_HEADS * HEAD_DIM
HYENA_WIDTH = D_MODEL // 4
POOL_WIDTH = D_MODEL - ATTN_WIDTH - HYENA_WIDTH
IN_WIDTH = ATTN_WIDTH + 2 * KV_WIDTH + 3 * HYENA_WIDTH + POOL_WIDTH
OFF_K = ATTN_WIDTH
OFF_V = OFF_K + KV_WIDTH
OFF_HY = OFF_V + KV_WIDTH
OFF_POOL = OFF_HY + 3 * HYENA_WIDTH

WINDOW = 128
BLOCK = 128
ROPE_THETA = 500000.0
ROPE_DIM = HEAD_DIM // 4
ROPE_HALF = ROPE_DIM // 2
NEG_BIG = -1e30
REMOVED = -3.0e38

SHORT_CONV = 3
FILTER_EMB = 33
FILTER_BANDS = (FILTER_EMB - 1) // 2
FILTER_ORDER = 64
DECAY_TARGET = 1e-2
SHORT_DECAY_PCT = 0.3
LONG_DECAY_PCT = 1.5

POOL_WINDOWS = (2, 4, 8, 16)
POOL_GROUP = POOL_WIDTH // len(POOL_WINDOWS)
OUT_GROUP = 128

N_EXPERTS = 64
TOP_K = 8
N_EXPERT_GROUPS = 8
EXPERTS_PER_GROUP = N_EXPERTS // N_EXPERT_GROUPS
TOPK_GROUPS = 4
EXPERT_HIDDEN = 512
SHARED_HIDDEN = 512
ROUTED_SCALE = 2.5

ALPHA = (2 * DEPTH) ** 0.25
LN_EPS = 1e-5
RMS_EPS = 1e-6

SUBLANES = 8
LANES = 128
VMEM_LIMIT = 56 * 1024 * 1024

FFT_N1 = 64
HY_CB = 128
MOE_MB = 512
HALF_D = D_MODEL // 2


def _cparams(sem, vmem=None):
    return pltpu.CompilerParams(dimension_semantics=sem, vmem_limit_bytes=vmem)


def _row_tile(t, pref):
    while t % pref:
        pref //= 2
    return pref


def _ln_rows(x, g, b):
    mu = jnp.mean(x, axis=-1, keepdims=True)
    xc = x - mu
    var = jnp.mean(xc * xc, axis=-1, keepdims=True)
    return xc * lax.rsqrt(var + LN_EPS) * g + b


def _pack_bf16_pair(x):
    h = x.shape[1] // 2
    hi = lax.bitcast_convert_type(x[:, :h].astype(BF16).astype(F32), jnp.uint32)
    lo = lax.bitcast_convert_type(x[:, h:].astype(BF16).astype(F32), jnp.uint32)
    return hi | (lo >> 16)


def _unpack_bf16_pair(pk):
    a = lax.bitcast_convert_type(pk & jnp.uint32(0xFFFF0000), F32).astype(BF16)
    b = lax.bitcast_convert_type(pk << 16, F32).astype(BF16)
    return a, b


def _ln_in_kernel(xp_ref, xs_ref, g_ref, b_ref, o_ref, ob_ref, *, n_prompt_blocks):
    i = pl.program_id(0)

    def emit(x):
        y = _ln_rows(x, g_ref[...], b_ref[...])
        o_ref[...] = y
        ob_ref[...] = y.astype(BF16)

    @pl.when(i < n_prompt_blocks)
    def _():
        emit(xp_ref[...])

    @pl.when(i >= n_prompt_blocks)
    def _():
        emit(xs_ref[...])


def _ln_in(xp, xs, g, b):
    tp, d = xp.shape
    ts = xs.shape[0]
    tm = _row_tile(math.gcd(tp, ts), 512)
    npb, nsb = tp // tm, ts // tm
    t = tp + ts
    return pl.pallas_call(
        functools.partial(_ln_in_kernel, n_prompt_blocks=npb),
        grid=(npb + nsb,),
        in_specs=[
            pl.BlockSpec((tm, d), lambda i: (jnp.minimum(i, npb - 1), 0)),
            pl.BlockSpec((tm, d), lambda i: (jnp.maximum(i - npb, 0), 0)),
            pl.BlockSpec((1, d), lambda i: (0, 0)),
            pl.BlockSpec((1, d), lambda i: (0, 0)),
        ],
        out_specs=[pl.BlockSpec((tm, d), lambda i: (i, 0)), pl.BlockSpec((tm, d), lambda i: (i, 0))],
        out_shape=[jax.ShapeDtypeStruct((t, d), F32), jax.ShapeDtypeStruct((t, d), BF16)],
        compiler_params=_cparams(("arbitrary",)),
        name="ln_in",
    )(xp, xs, g, b)


def _matmul_kernel(x_ref, w_ref, o_ref):
    o_ref[...] = jnp.dot(x_ref[...], w_ref[...], preferred_element_type=F32)


def _in_proj(xb, w):
    t, k = xb.shape
    n = w.shape[1]
    tm = _row_tile(t, 1024)
    tn = 512
    return pl.pallas_call(
        _matmul_kernel,
        grid=(t // tm, n // tn),
        in_specs=[pl.BlockSpec((tm, k), lambda i, j: (i, 0)), pl.BlockSpec((k, tn), lambda i, j: (0, j))],
        out_specs=pl.BlockSpec((tm, tn), lambda i, j: (i, j)),
        out_shape=jax.ShapeDtypeStruct((t, n), F32),
        compiler_params=_cparams(("arbitrary", "arbitrary")),
        name="in_proj",
    )(xb, w)


def _rope_tables(seq):
    pos = np.arange(-BLOCK, seq + BLOCK, dtype=np.float64)
    inv = ROPE_THETA ** (-np.arange(ROPE_HALF, dtype=np.float64) / ROPE_HALF)
    ang = pos[:, None] * inv[None, :]
    c = np.ones((pos.shape[0], HEAD_DIM))
    s = np.zeros((pos.shape[0], HEAD_DIM))
    c[:, :ROPE_HALF] = np.cos(ang)
    c[:, ROPE_HALF:ROPE_DIM] = np.cos(ang)
    s[:, :ROPE_HALF] = -np.sin(ang)
    s[:, ROPE_HALF:ROPE_DIM] = np.sin(ang)
    return jnp.asarray(c, F32), jnp.asarray(s, F32)


def _rope(x, c, s):
    lane = lax.broadcasted_iota(jnp.int32, x.shape, 1)
    partner = jnp.where(lane < ROPE_HALF, pltpu.roll(x, HEAD_DIM - ROPE_HALF, 1), pltpu.roll(x, ROPE_HALF, 1))
    return x * c + partner * s


def _attn_kernel(sink_ref, q_ref, kp_ref, ko_ref, kn_ref, vp_ref, vo_ref, vn_ref, ct_ref, st_ref, o_ref, *, nb):
    i = pl.program_id(1)
    base = pl.multiple_of(i * BLOCK, BLOCK)
    cw = ct_ref[pl.ds(base, 3 * BLOCK), :]
    sw = st_ref[pl.ds(base, 3 * BLOCK), :]
    cq, sq = cw[BLOCK:2 * BLOCK], sw[BLOCK:2 * BLOCK]
    kwin = jnp.concatenate([kp_ref[0], ko_ref[0], kn_ref[0]], axis=0)
    vwin = jnp.concatenate([vp_ref[0], vo_ref[0], vn_ref[0]], axis=0)
    q = q_ref[0]

    rows = Q_PER_KV * BLOCK
    rr = lax.broadcasted_iota(jnp.int32, (rows, 3 * BLOCK), 0) & (BLOCK - 1)
    cc = lax.broadcasted_iota(jnp.int32, (rows, 3 * BLOCK), 1)
    rel = cc - rr
    lo = jnp.where(i == 0, BLOCK, 0)
    hi = jnp.where(i == nb - 1, 2 * BLOCK, 3 * BLOCK)
    valid = (rel >= BLOCK - WINDOW) & (rel <= BLOCK + WINDOW) & (cc >= lo) & (cc < hi)
    rgrp = lax.broadcasted_iota(jnp.int32, (rows, 1), 0) // BLOCK

    for h in range(N_KV_HEADS):
        kh = _rope(kwin[:, h * HEAD_DIM:(h + 1) * HEAD_DIM], cw, sw).astype(BF16)
        vh = vwin[:, h * HEAD_DIM:(h + 1) * HEAD_DIM].astype(BF16)
        qs = []
        sink = jnp.zeros((rows, 1), F32)
        for g in range(Q_PER_KV):
            hq = h * Q_PER_KV + g
            qs.append(_rope(q[:, hq * HEAD_DIM:(hq + 1) * HEAD_DIM], cq, sq).astype(BF16))
            sink = jnp.where(rgrp == g, sink_ref[hq], sink)
        qg = jnp.concatenate(qs, axis=0)
        s = lax.dot_general(qg, kh, (((1,), (1,)), ((), ())), preferred_element_type=F32) * (HEAD_DIM ** -0.5)
        s = jnp.where(valid, s, NEG_BIG)
        m = jnp.maximum(jnp.max(s, axis=-1, keepdims=True), sink)
        p = jnp.exp(s - m)
        denom = jnp.sum(p, axis=-1, keepdims=True) + jnp.exp(sink - m)
        o = jnp.dot((p / denom).astype(BF16), vh, preferred_element_type=F32)
        for g in range(Q_PER_KV):
            hq = h * Q_PER_KV + g
            o_ref[0, :, hq * HEAD_DIM:(hq + 1) * HEAD_DIM] = o[g * BLOCK:(g + 1) * BLOCK]


def _attention(proj3, sink, ctab, stab):
    b, seq, _ = proj3.shape
    nb = seq // BLOCK
    kcol, vcol = OFF_K // KV_WIDTH, OFF_V // KV_WIDTH

    def kv_spec(col, shift):
        return pl.BlockSpec((1, BLOCK, KV_WIDTH), lambda bi, i: (bi, jnp.clip(i + shift, 0, nb - 1), col))

    return pl.pallas_call(
        functools.partial(_attn_kernel, nb=nb),
        grid=(b, nb),
        in_specs=[
            pl.BlockSpec(memory_space=pltpu.SMEM),
            pl.BlockSpec((1, BLOCK, ATTN_WIDTH), lambda bi, i: (bi, i, 0)),
            kv_spec(kcol, -1), kv_spec(kcol, 0), kv_spec(kcol, 1),
            kv_spec(vcol, -1), kv_spec(vcol, 0), kv_spec(vcol, 1),
            pl.BlockSpec(ctab.shape, lambda bi, i: (0, 0)),
            pl.BlockSpec(stab.shape, lambda bi, i: (0, 0)),
        ],
        out_specs=pl.BlockSpec((1, BLOCK, ATTN_WIDTH), lambda bi, i: (bi, i, 0)),
        out_shape=jax.ShapeDtypeStruct((b, seq, ATTN_WIDTH), F32),
        compiler_params=_cparams(("arbitrary", "arbitrary")),
        name="attention",
    )(sink, proj3, proj3, proj3, proj3, proj3, proj3, proj3, ctab, stab)


class _FftPlan:
    def __init__(self, seq):
        self.seq = seq
        self.n = 2 * seq
        self.n1 = FFT_N1
        self.n2 = self.n // self.n1
        self.nh = seq // self.n1
        self.k2 = self.n2 // 2 + 1
        self.k2p = -(-self.k2 // SUBLANES) * SUBLANES
        self.srow = 2 * self.n1 + SUBLANES
        n1 = np.arange(self.n1)
        n2 = np.arange(self.nh)
        k2 = np.arange(self.k2)
        npos = self.n1 * n2[None, None, :] + n1[:, None, None]
        ang = 2.0 * np.pi * k2[None, :, None] * npos / self.n
        f1 = np.zeros((self.n1, 2 * self.k2p, self.nh))
        f1[:, :self.k2] = np.cos(ang)
        f1[:, self.k2p:self.k2p + self.k2] = -np.sin(ang)
        wk = np.full((self.k2,), 2.0)
        wk[0] = 1.0
        wk[-1] = 1.0
        gm = np.zeros((self.n1, self.nh, 2 * self.k2p))
        angt = np.transpose(ang, (0, 2, 1))
        gm[:, :, :self.k2] = np.cos(angt) * wk / self.n
        gm[:, :, self.k2p:self.k2p + self.k2] = -np.sin(angt) * wk / self.n
        a2 = 2.0 * np.pi * np.outer(n1, n1) / self.n1
        c2, s2 = np.cos(a2), np.sin(a2)
        self.f1 = jnp.asarray(f1, F32)
        self.g = jnp.asarray(gm, F32)
        self.m2 = jnp.asarray(np.block([[c2, s2], [-s2, c2]]), F32)
        self.m2i = jnp.asarray(np.block([[c2, -s2], [s2, c2]]), F32)


def _dotf(a, b):
    return jnp.dot(a, b, precision=HIGHEST, preferred_element_type=F32)


def _fft_stage1(plan, src_ref, f1_ref, slab_ref):
    def body(n1, carry):
        rows = src_ref[pl.ds(n1, plan.nh, stride=plan.n1), :]
        res = _dotf(f1_ref[n1], rows)
        slab_ref[pl.ds(n1, plan.k2p, stride=plan.srow), :] = res[:plan.k2p]
        slab_ref[pl.ds(plan.n1 + n1, plan.k2p, stride=plan.srow), :] = res[plan.k2p:]
        return carry

    lax.fori_loop(0, plan.n1, body, 0)


def _hyena_filter_kernel(z_ref, w1_ref, b1_ref, fr_ref, w2_ref, b2_ref, w3f_ref, w3b_ref, dl_ref,
                         f1_ref, m2_ref, h_ref, srcf_ref, srcb_ref, slabf_ref, slabb_ref, *, plan):
    seq = plan.seq
    fr = fr_ref[0]
    hid = jnp.sin(fr * (_dotf(z_ref[...], w1_ref[0]) + b1_ref[0]))
    hid = jnp.sin(fr * (_dotf(hid, w2_ref[0]) + b2_ref[0]))
    row = lax.broadcasted_iota(jnp.int32, (seq, 1), 0)
    t = row.astype(F32) * (1.0 / (seq - 1))
    decay = jnp.exp(-t * jnp.abs(dl_ref[...]))
    srcf_ref[...] = _dotf(hid, w3f_ref[0]) * decay
    srcb_ref[...] = jnp.where(row == 0, 0.0, _dotf(hid, w3b_ref[0]) * decay)
    _fft_stage1(plan, srcf_ref, f1_ref, slabf_ref)
    _fft_stage1(plan, srcb_ref, f1_ref, slabb_ref)

    def body(k2, carry):
        off = pl.multiple_of(k2 * plan.srow, SUBLANES)
        xf = _dotf(m2_ref[...], slabf_ref[pl.ds(off, 2 * plan.n1), :])
        xb = _dotf(m2_ref[...], slabb_ref[pl.ds(off, 2 * plan.n1), :])
        h_ref[0, k2, :plan.n1, :] = xf[:plan.n1] + xb[:plan.n1]
        h_ref[0, k2, plan.n1:, :] = xf[plan.n1:] - xb[plan.n1:]
        return carry

    lax.fori_loop(0, plan.k2, body, 0)


def _hyena_filter_spectrum(plan, z, w1p, b1, freq, w2, b2, w3, deltas):
    depth = w3.shape[0]
    cb = HY_CB
    ncb = HYENA_WIDTH // cb
    seq = plan.seq
    lay = lambda shp: pl.BlockSpec((1,) + shp, lambda l, j: (l,) + (0,) * len(shp))
    return pl.pallas_call(
        functools.partial(_hyena_filter_kernel, plan=plan),
        grid=(depth, ncb),
        in_specs=[
            pl.BlockSpec(z.shape, lambda l, j: (0, 0)),
            lay(w1p.shape[1:]), lay(b1.shape[1:]), lay(freq.shape[1:]), lay(w2.shape[1:]), lay(b2.shape[1:]),
            pl.BlockSpec((1, FILTER_ORDER, cb), lambda l, j: (l, 0, j)),
            pl.BlockSpec((1, FILTER_ORDER, cb), lambda l, j: (l, 0, ncb + j)),
            pl.BlockSpec((1, cb), lambda l, j: (0, j)),
            pl.BlockSpec(plan.f1.shape, lambda l, j: (0, 0, 0)),
            pl.BlockSpec(plan.m2.shape, lambda l, j: (0, 0)),
        ],
        out_specs=pl.BlockSpec((1, plan.k2, 2 * plan.n1, cb), lambda l, j: (l, 0, 0, j)),
        out_shape=jax.ShapeDtypeStruct((depth, plan.k2, 2 * plan.n1, HYENA_WIDTH), F32),
        scratch_shapes=[pltpu.VMEM((seq, cb), F32), pltpu.VMEM((seq, cb), F32),
                        pltpu.VMEM((plan.k2p * plan.srow, cb), F32), pltpu.VMEM((plan.k2p * plan.srow, cb), F32)],
        compiler_params=_cparams(("arbitrary", "arbitrary"), VMEM_LIMIT),
        name="hyena_filter",
    )(z, w1p, b1, freq, w2, b2, w3, w3, deltas, plan.f1, plan.m2)


def _short_conv(u, w, b, seq):
    row = lax.broadcasted_iota(jnp.int32, (seq, 1), 0)
    prev = jnp.where(row >= 1, pltpu.roll(u, 1, 0), 0.0)
    nxt = jnp.where(row < seq - 1, pltpu.roll(u, seq - 1, 0), 0.0)
    return prev * w[0:1] + u * w[1:2] + nxt * w[2:3] + b


def _hyena_kernel(x0_ref, x1_ref, v_ref, w0_ref, w1_ref, wv_ref, b0_ref, b1_ref, bv_ref, skip_ref, h_ref,
                  f1_ref, m2_ref, m2i_ref, g_ref, o_ref, src_ref, gate_ref, slab_ref, *, plan):
    seq = plan.seq
    gate_ref[...] = _short_conv(x0_ref[0], w0_ref[...], b0_ref[...], seq)
    src_ref[...] = (_short_conv(v_ref[0], wv_ref[...], bv_ref[...], seq)
                    * _short_conv(x1_ref[0], w1_ref[...], b1_ref[...], seq))
    _fft_stage1(plan, src_ref, f1_ref, slab_ref)

    def stage2(k2, carry):
        off = pl.multiple_of(k2 * plan.srow, SUBLANES)
        x = _dotf(m2_ref[...], slab_ref[pl.ds(off, 2 * plan.n1), :])
        h = h_ref[0, k2]
        xr, xi = x[:plan.n1], x[plan.n1:]
        hr, hi = h[:plan.n1], h[plan.n1:]
        y = jnp.concatenate([xr * hr - xi * hi, xr * hi + xi * hr], axis=0)
        slab_ref[pl.ds(off, 2 * plan.n1), :] = _dotf(m2i_ref[...], y)
        return carry

    lax.fori_loop(0, plan.k2, stage2, 0)

    def stage3(n1, carry):
        ar = slab_ref[pl.ds(n1, plan.k2p, stride=plan.srow), :]
        ai = slab_ref[pl.ds(plan.n1 + n1, plan.k2p, stride=plan.srow), :]
        y = _dotf(g_ref[n1], jnp.concatenate([ar, ai], axis=0))
        o_ref[0, pl.ds(n1, plan.nh, stride=plan.n1), :] = y
        return carry

    lax.fori_loop(0, plan.n1, stage3, 0)
    o_ref[0] = (o_ref[0] + src_ref[...] * skip_ref[...]) * gate_ref[...]


def _hyena(plan, proj3, conv_w, conv_b, skip, hspec, layer):
    b, seq, _ = proj3.shape
    cb = HY_CB
    ncb = HYENA_WIDTH // cb
    c0 = OFF_HY // cb

    def u_spec(part):
        return pl.BlockSpec((1, seq, cb), lambda j, bi: (bi, 0, c0 + part * ncb + j))

    def w_spec(part):
        return pl.BlockSpec((SHORT_CONV, cb), lambda j, bi: (0, part * ncb + j))

    def b_spec(part):
        return pl.BlockSpec((1, cb), lambda j, bi: (0, part * ncb + j))

    const = lambda a: pl.BlockSpec(a.shape, lambda j, bi: (0,) * a.ndim)
    return pl.pallas_call(
        functools.partial(_hyena_kernel, plan=plan),
        grid=(ncb, b),
        in_specs=[
            u_spec(0), u_spec(1), u_spec(2), w_spec(0), w_spec(1), w_spec(2), b_spec(0), b_spec(1), b_spec(2),
            pl.BlockSpec((1, cb), lambda j, bi: (0, j)),
            pl.BlockSpec((1, plan.k2, 2 * plan.n1, cb), lambda j, bi: (layer, 0, 0, j)),
            const(plan.f1), const(plan.m2), const(plan.m2i), const(plan.g),
        ],
        out_specs=pl.BlockSpec((1, seq, cb), lambda j, bi: (bi, 0, j)),
        out_shape=jax.ShapeDtypeStruct((b, seq, HYENA_WIDTH), F32),
        scratch_shapes=[pltpu.VMEM((seq, cb), F32), pltpu.VMEM((seq, cb), F32),
                        pltpu.VMEM((plan.k2p * plan.srow, cb), F32)],
        compiler_params=_cparams(("arbitrary", "arbitrary"), VMEM_LIMIT),
        name="hyena",
    )(proj3, proj3, proj3, conv_w, conv_w, conv_w, conv_b, conv_b, conv_b, skip, hspec,
      plan.f1, plan.m2, plan.m2i, plan.g)


def _pool_kernel(u_ref, w_ref, sc_ref, o_ref, *, seq):
    row = lax.broadcasted_iota(jnp.int32, (seq, 1), 0)

    def back(x, s):
        return jnp.where(row >= s, pltpu.roll(x, s, 0), 0.0)

    def fwd(x, s):
        return jnp.where(row < seq - s, pltpu.roll(x, seq - s, 0), 0.0)

    for gi, win in enumerate(POOL_WINDOWS):
        half = win // 2
        u = u_ref[0, :, gi * POOL_GROUP:(gi + 1) * POOL_GROUP]
        ahead, behind, s = u, u, 1
        while s < half:
            ahead = ahead + fwd(ahead, s)
            behind = behind + back(behind, s)
            s *= 2
        total = ahead + back(behind, 1)
        cnt = (jnp.minimum(row + half, seq) - jnp.maximum(row - half, 0)).astype(F32)
        diff = total / cnt - u
        y = jnp.dot(diff.astype(BF16), w_ref[gi], preferred_element_type=F32)
        o_ref[0, :, gi * POOL_GROUP:(gi + 1) * POOL_GROUP] = y * sc_ref[:, gi * POOL_GROUP:(gi + 1) * POOL_GROUP]


def _pool(proj3, pool_w, pool_scale):
    b, seq, _ = proj3.shape
    return pl.pallas_call(
        functools.partial(_pool_kernel, seq=seq),
        grid=(b,),
        in_specs=[
            pl.BlockSpec((1, seq, POOL_WIDTH), lambda bi: (bi, 0, OFF_POOL // POOL_WIDTH)),
            pl.BlockSpec(pool_w.shape, lambda bi: (0, 0, 0)),
            pl.BlockSpec((1, POOL_WIDTH), lambda bi: (0, 0)),
        ],
        out_specs=pl.BlockSpec((1, seq, POOL_WIDTH), lambda bi: (bi, 0, 0)),
        out_shape=jax.ShapeDtypeStruct((b, seq, POOL_WIDTH), F32),
        compiler_params=_cparams(("arbitrary",), VMEM_LIMIT),
        name="pool",
    )(proj3, pool_w, pool_scale)


def _post_mixer_kernel(ya_ref, yh_ref, yp_ref, x_ref, gn_ref, w_ref, g_ref, b_ref, o_ref, opk_ref):
    parts = []
    for src in (ya_ref, yh_ref, yp_ref):
        for j in range(src.shape[1] // OUT_GROUP):
            c = src[:, j * OUT_GROUP:(j + 1) * OUT_GROUP]
            parts.append(c * lax.rsqrt(jnp.mean(c * c, axis=-1, keepdims=True) + RMS_EPS))
    yn = (jnp.concatenate(parts, axis=-1) * gn_ref[...]).astype(BF16)
    mix = jnp.dot(yn, w_ref[...], preferred_element_type=F32)
    x1 = _ln_rows(ALPHA * x_ref[...] + mix, g_ref[...], b_ref[...])
    o_ref[...] = x1
    opk_ref[...] = _pack_bf16_pair(x1)


def _post_mixer(ya, yh, yp, x, gn, w_out, g, b):
    t, d = x.shape
    tm = _row_tile(t, 512)
    row = lambda w: pl.BlockSpec((tm, w), lambda i: (i, 0))
    vec = pl.BlockSpec((1, d), lambda i: (0, 0))
    return pl.pallas_call(
        _post_mixer_kernel,
        grid=(t // tm,),
        in_specs=[row(ATTN_WIDTH), row(HYENA_WIDTH), row(POOL_WIDTH), row(d), vec,
                  pl.BlockSpec((d, d), lambda i: (0, 0)), vec, vec],
        out_specs=[row(d), row(d // 2)],
        out_shape=[jax.ShapeDtypeStruct((t, d), F32), jax.ShapeDtypeStruct((t, d // 2), jnp.uint32)],
        compiler_params=_cparams(("arbitrary",), VMEM_LIMIT),
        name="post_mixer",
    )(ya, yh, yp, x, gn, w_out, g, b)


def _router_kernel(x_ref, rw_ref, bias_ref, tri_ref, eidx_ref, gate_ref, rank_ref, cnt_ref, carry_ref):
    tm = x_ref.shape[0]

    @pl.when(pl.program_id(0) == 0)
    def _():
        carry_ref[...] = jnp.zeros_like(carry_ref)

    logits = lax.dot_general(rw_ref[...], x_ref[...], (((1,), (1,)), ((), ())),
                             precision=HIGHEST, preferred_element_type=F32)
    scores = jax.nn.sigmoid(logits)
    choice = scores + bias_ref[...]
    sub = lax.broadcasted_iota(jnp.int32, (EXPERTS_PER_GROUP, tm), 0)
    far = jnp.int32(N_EXPERTS)

    def first_argmax(v):
        m = jnp.max(v, axis=0, keepdims=True)
        return m, jnp.min(jnp.where(v == m, sub, far), axis=0, keepdims=True)

    tiles, stiles, gscore = [], [], []
    for g in range(N_EXPERT_GROUPS):
        c = choice[g * EXPERTS_PER_GROUP:(g + 1) * EXPERTS_PER_GROUP]
        tiles.append(c)
        stiles.append(scores[g * EXPERTS_PER_GROUP:(g + 1) * EXPERTS_PER_GROUP])
        m1, i1 = first_argmax(c)
        m2 = jnp.max(jnp.where(sub == i1, REMOVED, c), axis=0, keepdims=True)
        gscore.append(m1 + m2)
    cur = jnp.concatenate(gscore, axis=0)
    gsel = jnp.zeros(cur.shape, F32)
    for _ in range(TOPK_GROUPS):
        _, ig = first_argmax(cur)
        hit = sub == ig
        gsel = jnp.where(hit, 1.0, gsel)
        cur = jnp.where(hit, REMOVED, cur)

    eids = [sub + g * EXPERTS_PER_GROUP for g in range(N_EXPERT_GROUPS)]
    masked = [jnp.where(gsel[g:g + 1] > 0.5, tiles[g], NEG_BIG) for g in range(N_EXPERT_GROUPS)]
    picked = [jnp.zeros((EXPERTS_PER_GROUP, tm), F32) for _ in range(N_EXPERT_GROUPS)]
    idxs, sels = [], []
    for _ in range(TOP_K):
        m = functools.reduce(jnp.maximum, [jnp.max(v, axis=0, keepdims=True) for v in masked])
        idx = functools.reduce(jnp.minimum, [jnp.min(jnp.where(v == m, e, far), axis=0, keepdims=True)
                                             for v, e in zip(masked, eids)])
        sc = jnp.zeros((1, tm), F32)
        for g in range(N_EXPERT_GROUPS):
            hit = eids[g] == idx
            sc = sc + jnp.sum(jnp.where(hit, stiles[g], 0.0), axis=0, keepdims=True)
            masked[g] = jnp.where(hit, REMOVED, masked[g])
            picked[g] = jnp.where(hit, 1.0, picked[g])
        idxs.append(idx)
        sels.append(sc)
    total = functools.reduce(lambda a, c: a + c, sels)
    for k in range(TOP_K):
        eidx_ref[k:k + 1, :] = idxs[k]
        gate_ref[k:k + 1, :] = sels[k] / total * ROUTED_SCALE

    sel = jnp.concatenate(picked, axis=0)
    before = jnp.dot(sel.astype(BF16), tri_ref[...], preferred_element_type=F32) + carry_ref[:, 0:1]
    for k in range(TOP_K):
        r = jnp.zeros((1, tm), F32)
        for g in range(N_EXPERT_GROUPS):
            r = r + jnp.sum(jnp.where(eids[g] == idxs[k], before[g * EXPERTS_PER_GROUP:(g + 1) * EXPERTS_PER_GROUP], 0.0),
                            axis=0, keepdims=True)
        rank_ref[k:k + 1, :] = r.astype(jnp.int32)
    carry_ref[...] = carry_ref[...] + jnp.sum(sel, axis=1, keepdims=True)
    cnt_ref[...] = carry_ref[...]


def _router(x1, rw_t, bias):
    t, d = x1.shape
    tm = _row_tile(t, 512)
    tri = jnp.asarray(np.triu(np.ones((tm, tm)), 1), BF16)
    bias_b = jnp.broadcast_to(bias[:, None], (N_EXPERTS, tm)).astype(F32)
    kt = lambda dt: jax.ShapeDtypeStruct((TOP_K, t), dt)
    return pl.pallas_call(
        _router_kernel,
        grid=(t // tm,),
        in_specs=[
            pl.BlockSpec((tm, d), lambda i: (i, 0)),
            pl.BlockSpec((N_EXPERTS, d), lambda i: (0, 0)),
            pl.BlockSpec((N_EXPERTS, tm), lambda i: (0, 0)),
            pl.BlockSpec((tm, tm), lambda i: (0, 0)),
        ],
        out_specs=[pl.BlockSpec((TOP_K, tm), lambda i: (0, i))] * 3 + [pl.BlockSpec((N_EXPERTS, LANES), lambda i: (0, 0))],
        out_shape=[kt(jnp.int32), kt(F32), kt(jnp.int32), jax.ShapeDtypeStruct((N_EXPERTS, LANES), F32)],
        scratch_shapes=[pltpu.VMEM((N_EXPERTS, LANES), F32)],
        compiler_params=_cparams(("arbitrary",)),
        name="router",
    )(x1, rw_t, bias_b, tri)


def _dispatch_plan(eidx_t, rank_t, counts, n_blocks):
    mb = MOE_MB
    cnt = counts[:, 0].astype(jnp.int32)
    padded = (cnt + mb - 1) // mb * mb
    pend = jnp.cumsum(padded)
    pstart = pend - padded
    experts = jnp.arange(N_EXPERTS, dtype=jnp.int32)
    slot_t = rank_t + jnp.sum(jnp.where(eidx_t[:, :, None] == experts, pstart, 0), axis=-1)
    block_end = pend // mb
    blocks = jnp.arange(n_blocks, dtype=jnp.int32)
    block_expert = jnp.minimum(jnp.sum((block_end[None, :] <= blocks[:, None]).astype(jnp.int32), axis=1),
                               N_EXPERTS - 1).astype(jnp.int32)
    n_used = block_end[-1:].astype(jnp.int32)
    last_block = (block_end - 1).astype(jnp.int32)
    has_rows = (padded > 0).astype(jnp.int32)
    return slot_t.astype(jnp.int32), block_expert, n_used, last_block, has_rows


def _dispatch_kernel(last_ref, has_ref, nused_ref, slot_ref, x_ref, xs_ref, zero_ref, zsem, sem, *, n_blocks):
    mb = MOE_MB
    tm = x_ref.shape[0]

    def zero_copy(block):
        return pltpu.make_async_copy(zero_ref, xs_ref.at[pl.ds(pl.multiple_of(block * mb, mb), mb)], zsem)

    @pl.when(pl.program_id(0) == 0)
    def _():
        zero_ref[...] = jnp.zeros_like(zero_ref)

        def start_e(e, c):
            @pl.when(has_ref[e] > 0)
            def _():
                zero_copy(last_ref[e]).start()
            return c

        def wait_e(e, c):
            @pl.when(has_ref[e] > 0)
            def _():
                zero_copy(last_ref[e]).wait()
            return c

        def start_b(blk, c):
            zero_copy(blk).start()
            return c

        def wait_b(blk, c):
            zero_copy(blk).wait()
            return c

        lax.fori_loop(0, N_EXPERTS, start_e, 0)
        lax.fori_loop(nused_ref[0], n_blocks, start_b, 0)
        lax.fori_loop(0, N_EXPERTS, wait_e, 0)
        lax.fori_loop(nused_ref[0], n_blocks, wait_b, 0)

    def row_copy(r, k):
        return pltpu.make_async_copy(x_ref.at[pl.ds(r, 1)], xs_ref.at[pl.ds(slot_ref[k, r], 1)], sem)

    def issue(r, c):
        for k in range(TOP_K):
            row_copy(r, k).start()
        return c

    lax.fori_loop(0, tm, issue, 0)
    for k in range(TOP_K):
        pltpu.make_async_copy(x_ref, xs_ref.at[pl.ds(0, tm)], sem).wait()


def _dispatch(xpk, slot_t, n_used, last_block, has_rows, n_blocks):
    t, hw = xpk.shape
    tm = _row_tile(t, 256)
    grid_spec = pltpu.PrefetchScalarGridSpec(
        num_scalar_prefetch=3,
        grid=(t // tm,),
        in_specs=[
            pl.BlockSpec((TOP_K, tm), lambda i, *_: (0, i), memory_space=pltpu.SMEM),
            pl.BlockSpec((tm, hw), lambda i, *_: (i, 0)),
        ],
        out_specs=pl.BlockSpec(memory_space=pl.ANY),
        scratch_shapes=[pltpu.VMEM((MOE_MB, hw), jnp.uint32), pltpu.SemaphoreType.DMA, pltpu.SemaphoreType.DMA],
    )
    return pl.pallas_call(
        functools.partial(_dispatch_kernel, n_blocks=n_blocks),
        grid_spec=grid_spec,
        out_shape=jax.ShapeDtypeStruct((n_blocks * MOE_MB, hw), jnp.uint32),
        compiler_params=_cparams(("arbitrary",)),
        name="dispatch",
    )(last_block, has_rows, n_used, slot_t, xpk)


def _swiglu_packed(pk, wg, wu, wd):
    a, b = _unpack_bf16_pair(pk)
    gate = jnp.dot(a, wg[:HALF_D], preferred_element_type=F32) + jnp.dot(b, wg[HALF_D:], preferred_element_type=F32)
    up = jnp.dot(a, wu[:HALF_D], preferred_element_type=F32) + jnp.dot(b, wu[HALF_D:], preferred_element_type=F32)
    hid = (gate * jax.nn.sigmoid(gate) * up).astype(BF16)
    return jnp.dot(hid, wd, preferred_element_type=F32)


def _experts_kernel(be_ref, nused_ref, xs_ref, wg_ref, wu_ref, wd_ref, y_ref):
    i = pl.program_id(0)

    @pl.when(i < nused_ref[0])
    def _():
        y_ref[...] = _swiglu_packed(xs_ref[...], wg_ref[0], wu_ref[0], wd_ref[0])

    @pl.when(i >= nused_ref[0])
    def _():
        y_ref[...] = jnp.zeros_like(y_ref)


def _experts(xs, block_expert, n_used, wg, wu, wd, n_blocks):
    hw = xs.shape[1]
    d = wd.shape[2]
    grid_spec = pltpu.PrefetchScalarGridSpec(
        num_scalar_prefetch=2,
        grid=(n_blocks,),
        in_specs=[
            pl.BlockSpec((MOE_MB, hw), lambda i, be, nu: (i, 0)),
            pl.BlockSpec((1,) + wg.shape[1:], lambda i, be, nu: (be[i], 0, 0)),
            pl.BlockSpec((1,) + wu.shape[1:], lambda i, be, nu: (be[i], 0, 0)),
            pl.BlockSpec((1,) + wd.shape[1:], lambda i, be, nu: (be[i], 0, 0)),
        ],
        out_specs=pl.BlockSpec((MOE_MB, d), lambda i, be, nu: (i, 0)),
    )
    return pl.pallas_call(
        _experts_kernel,
        grid_spec=grid_spec,
        out_shape=jax.ShapeDtypeStruct((n_blocks * MOE_MB, d), F32),
        compiler_params=_cparams(("arbitrary",), VMEM_LIMIT),
        name="experts",
    )(block_expert, n_used, xs, wg, wu, wd)


def _combine_kernel(slot_ref, gates_ref, x_ref, xpk_ref, ys_ref, wg_ref, wu_ref, wd_ref, g_ref, b_ref,
                    *rest, n_prompt_blocks, final):
    if final:
        op_ref, os_ref, buf_ref, sem = rest
    else:
        o_ref, ob_ref, buf_ref, sem = rest
    tm = x_ref.shape[0]

    def issue(r, c):
        for k in range(TOP_K):
            pltpu.make_async_copy(ys_ref.at[pl.ds(slot_ref[k, r], 1)], buf_ref.at[k, pl.ds(r, 1)], sem).start()
        return c

    lax.fori_loop(0, tm, issue, 0)
    shared = _swiglu_packed(xpk_ref[...], wg_ref[...], wu_ref[...], wd_ref[...])
    for k in range(TOP_K):
        pltpu.make_async_copy(ys_ref.at[pl.ds(0, tm)], buf_ref.at[k], sem).wait()
    gates = gates_ref[...]
    routed = gates[:, 0:1] * buf_ref[0]
    for k in range(1, TOP_K):
        routed = routed + gates[:, k:k + 1] * buf_ref[k]
    y = _ln_rows(ALPHA * x_ref[...] + (routed + shared), g_ref[...], b_ref[...])
    if final:
        i = pl.program_id(0)

        @pl.when(i < n_prompt_blocks)
        def _():
            op_ref[...] = y

        @pl.when(i >= n_prompt_blocks)
        def _():
            os_ref[...] = y
    else:
        o_ref[...] = y
        ob_ref[...] = y.astype(BF16)


def _combine(slot_t, gates, x1, xpk, ys, wg, wu, wd, g, b, t_prompt, final):
    t, d = x1.shape
    tm = _row_tile(math.gcd(t_prompt, t - t_prompt), 128)
    npb = t_prompt // tm
    nsb = (t - t_prompt) // tm
    vec = pl.BlockSpec((1, d), lambda i: (0, 0))
    full = lambda a: pl.BlockSpec(a.shape, lambda i: (0,) * a.ndim)
    if final:
        out_specs = [pl.BlockSpec((tm, d), lambda i: (jnp.minimum(i, npb - 1), 0)),
                     pl.BlockSpec((tm, d), lambda i: (jnp.maximum(i - npb, 0), 0))]
        out_shape = [jax.ShapeDtypeStruct((t_prompt, d), F32), jax.ShapeDtypeStruct((t - t_prompt, d), F32)]
    else:
        out_specs = [pl.BlockSpec((tm, d), lambda i: (i, 0)), pl.BlockSpec((tm, d), lambda i: (i, 0))]
        out_shape = [jax.ShapeDtypeStruct((t, d), F32), jax.ShapeDtypeStruct((t, d), BF16)]
    return pl.pallas_call(
        functools.partial(_combine_kernel, n_prompt_blocks=npb, final=final),
        grid=(npb + nsb,),
        in_specs=[
            pl.BlockSpec((TOP_K, tm), lambda i: (0, i), memory_space=pltpu.SMEM),
            pl.BlockSpec((tm, TOP_K), lambda i: (i, 0)),
            pl.BlockSpec((tm, d), lambda i: (i, 0)),
            pl.BlockSpec((tm, d // 2), lambda i: (i, 0)),
            pl.BlockSpec(memory_space=pl.ANY),
            full(wg), full(wu), full(wd), vec, vec,
        ],
        out_specs=out_specs,
        out_shape=out_shape,
        scratch_shapes=[pltpu.VMEM((TOP_K, tm, d), F32), pltpu.SemaphoreType.DMA],
        compiler_params=_cparams(("arbitrary",), VMEM_LIMIT),
        name="combine",
    )(slot_t, gates, x1, xpk, ys, wg, wu, wd, g, b)


def _filter_features(seq):
    t = np.linspace(0.0, 1.0, seq)[:, None]
    w = 2.0 * np.pi * np.arange(seq) / seq
    bands = np.linspace(1e-4, FILTER_BANDS - 1, FILTER_BANDS)
    ang = w[:, None] * bands[None, :]
    z = np.zeros((seq, FILTER_ORDER))
    z[:, :FILTER_EMB] = np.concatenate([t, np.cos(ang), -np.sin(ang)], axis=-1)
    return jnp.asarray(z, F32)


def _decay_rates():
    max_decay = math.log(DECAY_TARGET) / SHORT_DECAY_PCT
    min_decay = math.log(DECAY_TARGET) / LONG_DECAY_PCT
    return jnp.asarray(np.linspace(min_decay, max_decay, HYENA_WIDTH)[None, :], F32)


def kernel(x_prompt, x_sample, ln_in_g, ln_in_b, w_in, attn_sink, hy_conv_w, hy_conv_b, hy_f_w1, hy_f_b1,
           hy_f_freq, hy_f_w2, hy_f_b2, hy_f_w3, hy_skip, pool_w, pool_scale, out_norm_g, w_out, ln1_g, ln1_b,
           router_w, router_bias, exp_w_gate, exp_w_up, exp_w_down, sh_w_gate, sh_w_up, sh_w_down, ln2_g, ln2_b):
    bp, seq, d = x_prompt.shape
    bs = x_sample.shape[0]
    assert x_sample.shape[1] == seq and d == D_MODEL and seq % (FFT_N1 * SUBLANES) == 0
    nbatch = bp + bs
    t_prompt = bp * seq
    t = nbatch * seq
    n_blocks = t * TOP_K // MOE_MB + N_EXPERTS
    assert (t * TOP_K) % MOE_MB == 0

    plan = _FftPlan(seq)
    ctab, stab = _rope_tables(seq)
    w1p = jnp.pad(hy_f_w1, ((0, 0), (0, FILTER_ORDER - FILTER_EMB), (0, 0)))
    hspec = _hyena_filter_spectrum(plan, _filter_features(seq), w1p, hy_f_b1[:, None, :], hy_f_freq[:, None, :],
                                   hy_f_w2, hy_f_b2[:, None, :], hy_f_w3, _decay_rates())

    row = lambda v: v.reshape(1, -1)
    x, xb = _ln_in(x_prompt.reshape(t_prompt, d), x_sample.reshape(bs * seq, d), row(ln_in_g), row(ln_in_b))
    outs = None
    for l in range(DEPTH):
        proj3 = _in_proj(xb, w_in[l].astype(BF16)).reshape(nbatch, seq, IN_WIDTH)
        ya = _attention(proj3, attn_sink[l], ctab, stab)
        yh = _hyena(plan, proj3, hy_conv_w[l], row(hy_conv_b[l]), row(hy_skip[l]), hspec, l)
        yp = _pool(proj3, pool_w[l].astype(BF16), row(pool_scale[l]))
        x1, xpk = _post_mixer(ya.reshape(t, ATTN_WIDTH), yh.reshape(t, HYENA_WIDTH), yp.reshape(t, POOL_WIDTH), x,
                              row(out_norm_g[l]), w_out[l].astype(BF16), row(ln1_g[l]), row(ln1_b[l]))
        eidx_t, gate_t, rank_t, counts = _router(x1, router_w[l].T, router_bias[l])
        slot_t, block_expert, n_used, last_block, has_rows = _dispatch_plan(eidx_t, rank_t, counts, n_blocks)
        xs = _dispatch(xpk, slot_t, n_used, last_block, has_rows, n_blocks)
        ys = _experts(xs, block_expert, n_used, exp_w_gate[l].astype(BF16), exp_w_up[l].astype(BF16),
                      exp_w_down[l].astype(BF16), n_blocks)
        res = _combine(slot_t, gate_t.T, x1, xpk, ys, sh_w_gate[l].astype(BF16), sh_w_up[l].astype(BF16),
                       sh_w_down[l].astype(BF16), row(ln2_g[l]), row(ln2_b[l]), t_prompt, final=(l == DEPTH - 1))
        if l == DEPTH - 1:
            outs = res
        else:
            x, xb = res
    return outs[0].reshape(bp, seq, d), outs[1].reshape(bs, seq, d)
```

```python
import functools
import math

import jax
import jax.numpy as jnp
import numpy as np
from jax import lax
from jax.experimental import pallas as pl
from jax.experimental.pallas import tpu as pltpu

F32 = jnp.float32
BF16 = jnp.bfloat16
HIGHEST = lax.Precision.HIGHEST

D_MODEL = 2048
DEPTH = 2
HEAD_DIM = 128
ATTN_WIDTH = D_MODEL // 2
N_Q_HEADS = ATTN_WIDTH // HEAD_DIM
N_KV_HEADS = 2
Q_PER_KV = N_Q_HEADS // N_KV_HEADS
KV_WIDTH = N_KV_HEADS * HEAD_DIM
HYENA_WIDTH = D_MODEL // 4
POOL_WIDTH = D_MODEL - ATTN_WIDTH - HYENA_WIDTH
IN_WIDTH = ATTN_WIDTH + 2 * KV_WIDTH + 3 * HYENA_WIDTH + POOL_WIDTH
OFF_K = ATTN_WIDTH
OFF_V = OFF_K + KV_WIDTH
OFF_HY = OFF_V + KV_WIDTH
OFF_POOL = OFF_HY + 3 * HYENA_WIDTH

WINDOW = 128
BLOCK = 128
ROPE_THETA = 500000.0
ROPE_DIM = HEAD_DIM // 4
ROPE_HALF = ROPE_DIM // 2
NEG_BIG = -1e30
REMOVED = -3.0e38

SHORT_CONV = 3
FILTER_EMB = 33
FILTER_BANDS = (FILTER_EMB - 1) // 2
FILTER_ORDER = 64
DECAY_TARGET = 1e-2
SHORT_DECAY_PCT = 0.3
LONG_DECAY_PCT = 1.5

POOL_WINDOWS = (2, 4, 8, 16)
POOL_GROUP = POOL_WIDTH // len(POOL_WINDOWS)
OUT_GROUP = 128

N_EXPERTS = 64
TOP_K = 8
N_EXPERT_GROUPS = 8
EXPERTS_PER_GROUP = N_EXPERTS // N_EXPERT_GROUPS
TOPK_GROUPS = 4
EXPERT_HIDDEN = 512
SHARED_HIDDEN = 512
ROUTED_SCALE = 2.5

ALPHA = (2 * DEPTH) ** 0.25
LN_EPS = 1e-5
RMS_EPS = 1e-6

SUBLANES = 8
LANES = 128
VMEM_LIMIT = 56 * 1024 * 1024

FFT_N1 = 64
HY_CB = 128
MOE_MB = 512
HALF_D = D_MODEL // 2


def _cparams(sem, vmem=None):
    return pltpu.CompilerParams(dimension_semantics=sem, vmem_limit_bytes=vmem)


def _row_tile(t, pref):
    while t % pref:
        pref //= 2
    return pref


def _ln_rows(x, g, b):
    mu = jnp.mean(x, axis=-1, keepdims=True)
    xc = x - mu
    var = jnp.mean(xc * xc, axis=-1, keepdims=True)
    return xc * lax.rsqrt(var + LN_EPS) * g + b


def _pack_bf16_pair(x):
    h = x.shape[1] // 2
    hi = lax.bitcast_convert_type(x[:, :h].astype(BF16).astype(F32), jnp.uint32)
    lo = lax.bitcast_convert_type(x[:, h:].astype(BF16).astype(F32), jnp.uint32)
    return hi | (lo >> 16)


def _unpack_bf16_pair(pk):
    a = lax.bitcast_convert_type(pk & jnp.uint32(0xFFFF0000), F32).astype(BF16)
    b = lax.bitcast_convert_type(pk << 16, F32).astype(BF16)
    return a, b


def _ln_in_kernel(xp_ref, xs_ref, g_ref, b_ref, o_ref, ob_ref, *, n_prompt_blocks):
    i = pl.program_id(0)

    def emit(x):
        y = _ln_rows(x, g_ref[...], b_ref[...])
        o_ref[...] = y
        ob_ref[...] = y.astype(BF16)

    @pl.when(i < n_prompt_blocks)
    def _():
        emit(xp_ref[...])

    @pl.when(i >= n_prompt_blocks)
    def _():
        emit(xs_ref[...])


def _ln_in(xp, xs, g, b):
    tp, d = xp.shape
    ts = xs.shape[0]
    tm = _row_tile(math.gcd(tp, ts), 512)
    npb, nsb = tp // tm, ts // tm
    t = tp + ts
    return pl.pallas_call(
        functools.partial(_ln_in_kernel, n_prompt_blocks=npb),
        grid=(npb + nsb,),
        in_specs=[
            pl.BlockSpec((tm, d), lambda i: (jnp.minimum(i, npb - 1), 0)),
            pl.BlockSpec((tm, d), lambda i: (jnp.maximum(i - npb, 0), 0)),
            pl.BlockSpec((1, d), lambda i: (0, 0)),
            pl.BlockSpec((1, d), lambda i: (0, 0)),
        ],
        out_specs=[pl.BlockSpec((tm, d), lambda i: (i, 0)), pl.BlockSpec((tm, d), lambda i: (i, 0))],
        out_shape=[jax.ShapeDtypeStruct((t, d), F32), jax.ShapeDtypeStruct((t, d), BF16)],
        compiler_params=_cparams(("arbitrary",)),
        name="ln_in",
    )(xp, xs, g, b)


def _matmul_kernel(x_ref, w_ref, o_ref):
    o_ref[...] = jnp.dot(x_ref[...], w_ref[...], preferred_element_type=F32)


def _in_proj(xb, w):
    t, k = xb.shape
    n = w.shape[1]
    tm = _row_tile(t, 1024)
    tn = 512
    return pl.pallas_call(
        _matmul_kernel,
        grid=(t // tm, n // tn),
        in_specs=[pl.BlockSpec((tm, k), lambda i, j: (i, 0)), pl.BlockSpec((k, tn), lambda i, j: (0, j))],
        out_specs=pl.BlockSpec((tm, tn), lambda i, j: (i, j)),
        out_shape=jax.ShapeDtypeStruct((t, n), F32),
        compiler_params=_cparams(("arbitrary", "arbitrary")),
        name="in_proj",
    )(xb, w)


def _rope_tables(seq):
    pos = np.arange(-BLOCK, seq + BLOCK, dtype=np.float64)
    inv = ROPE_THETA ** (-np.arange(ROPE_HALF, dtype=np.float64) / ROPE_HALF)
    ang = pos[:, None] * inv[None, :]
    c = np.ones((pos.shape[0], HEAD_DIM))
    s = np.zeros((pos.shape[0], HEAD_DIM))
    c[:, :ROPE_HALF] = np.cos(ang)
    c[:, ROPE_HALF:ROPE_DIM] = np.cos(ang)
    s[:, :ROPE_HALF] = -np.sin(ang)
    s[:, ROPE_HALF:ROPE_DIM] = np.sin(ang)
    return jnp.asarray(c, F32), jnp.asarray(s, F32)


def _rope(x, c, s):
    lane = lax.broadcasted_iota(jnp.int32, x.shape, 1)
    partner = jnp.where(lane < ROPE_HALF, pltpu.roll(x, HEAD_DIM - ROPE_HALF, 1), pltpu.roll(x, ROPE_HALF, 1))
    return x * c + partner * s


def _attn_kernel(sink_ref, q_ref, kp_ref, ko_ref, kn_ref, vp_ref, vo_ref, vn_ref, ct_ref, st_ref, o_ref, *, nb):
    i = pl.program_id(1)
    base = pl.multiple_of(i * BLOCK, BLOCK)
    cw = ct_ref[pl.ds(base, 3 * BLOCK), :]
    sw = st_ref[pl.ds(base, 3 * BLOCK), :]
    cq, sq = cw[BLOCK:2 * BLOCK], sw[BLOCK:2 * BLOCK]
    kwin = jnp.concatenate([kp_ref[0], ko_ref[0], kn_ref[0]], axis=0)
    vwin = jnp.concatenate([vp_ref[0], vo_ref[0], vn_ref[0]], axis=0)
    q = q_ref[0]

    rows = Q_PER_KV * BLOCK
    rr = lax.broadcasted_iota(jnp.int32, (rows, 3 * BLOCK), 0) & (BLOCK - 1)
    cc = lax.broadcasted_iota(jnp.int32, (rows, 3 * BLOCK), 1)
    rel = cc - rr
    lo = jnp.where(i == 0, BLOCK, 0)
    hi = jnp.where(i == nb - 1, 2 * BLOCK, 3 * BLOCK)
    valid = (rel >= BLOCK - WINDOW) & (rel <= BLOCK + WINDOW) & (cc >= lo) & (cc < hi)
    rgrp = lax.broadcasted_iota(jnp.int32, (rows, 1), 0) // BLOCK

    for h in range(N_KV_HEADS):
        kh = _rope(kwin[:, h * HEAD_DIM:(h + 1) * HEAD_DIM], cw, sw).astype(BF16)
        vh = vwin[:, h * HEAD_DIM:(h + 1) * HEAD_DIM].astype(BF16)
        qs = []
        sink = jnp.zeros((rows, 1), F32)
        for g in range(Q_PER_KV):
            hq = h * Q_PER_KV + g
            qs.append(_rope(q[:, hq * HEAD_DIM:(hq + 1) * HEAD_DIM], cq, sq).astype(BF16))
            sink = jnp.where(rgrp == g, sink_ref[hq], sink)
        qg = jnp.concatenate(qs, axis=0)
        s = lax.dot_general(qg, kh, (((1,), (1,)), ((), ())), preferred_element_type=F32) * (HEAD_DIM ** -0.5)
        s = jnp.where(valid, s, NEG_BIG)
        m = jnp.maximum(jnp.max(s, axis=-1, keepdims=True), sink)
        p = jnp.exp(s - m)
        denom = jnp.sum(p, axis=-1, keepdims=True) + jnp.exp(sink - m)
        o = jnp.dot((p / denom).astype(BF16), vh, preferred_element_type=F32)
        for g in range(Q_PER_KV):
            hq = h * Q_PER_KV + g
            o_ref[0, :, hq * HEAD_DIM:(hq + 1) * HEAD_DIM] = o[g * BLOCK:(g + 1) * BLOCK]


def _attention(proj3, sink, ctab, stab):
    b, seq, _ = proj3.shape
    nb = seq // BLOCK
    kcol, vcol = OFF_K // KV_WIDTH, OFF_V // KV_WIDTH

    def kv_spec(col, shift):
        return pl.BlockSpec((1, BLOCK, KV_WIDTH), lambda bi, i: (bi, jnp.clip(i + shift, 0, nb - 1), col))

    return pl.pallas_call(
        functools.partial(_attn_kernel, nb=nb),
        grid=(b, nb),
        in_specs=[
            pl.BlockSpec(memory_space=pltpu.SMEM),
            pl.BlockSpec((1, BLOCK, ATTN_WIDTH), lambda bi, i: (bi, i, 0)),
            kv_spec(kcol, -1), kv_spec(kcol, 0), kv_spec(kcol, 1),
            kv_spec(vcol, -1), kv_spec(vcol, 0), kv_spec(vcol, 1),
            pl.BlockSpec(ctab.shape, lambda bi, i: (0, 0)),
            pl.BlockSpec(stab.shape, lambda bi, i: (0, 0)),
        ],
        out_specs=pl.BlockSpec((1, BLOCK, ATTN_WIDTH), lambda bi, i: (bi, i, 0)),
        out_shape=jax.ShapeDtypeStruct((b, seq, ATTN_WIDTH), F32),
        compiler_params=_cparams(("arbitrary", "arbitrary")),
        name="attention",
    )(sink, proj3, proj3, proj3, proj3, proj3, proj3, proj3, ctab, stab)


class _FftPlan:
    def __init__(self, seq):
        self.seq = seq
        self.n = 2 * seq
        self.n1 = FFT_N1
        self.n2 = self.n // self.n1
        self.nh = seq // self.n1
        self.k2 = self.n2 // 2 + 1
        self.k2p = -(-self.k2 // SUBLANES) * SUBLANES
        self.srow = 2 * self.n1 + SUBLANES
        n1 = np.arange(self.n1)
        n2 = np.arange(self.nh)
        k2 = np.arange(self.k2)
        npos = self.n1 * n2[None, None, :] + n1[:, None, None]
        ang = 2.0 * np.pi * k2[None, :, None] * npos / self.n
        f1 = np.zeros((self.n1, 2 * self.k2p, self.nh))
        f1[:, :self.k2] = np.cos(ang)
        f1[:, self.k2p:self.k2p + self.k2] = -np.sin(ang)
        wk = np.full((self.k2,), 2.0)
        wk[0] = 1.0
        wk[-1] = 1.0
        gm = np.zeros((self.n1, self.nh, 2 * self.k2p))
        angt = np.transpose(ang, (0, 2, 1))
        gm[:, :, :self.k2] = np.cos(angt) * wk / self.n
        gm[:, :, self.k2p:self.k2p + self.k2] = -np.sin(angt) * wk / self.n
        a2 = 2.0 * np.pi * np.outer(n1, n1) / self.n1
        c2, s2 = np.cos(a2), np.sin(a2)
        self.f1 = _split3_lhs(f1)
        self.g = _split3_lhs(gm)
        self.m2 = _split3_lhs(np.block([[c2, s2], [-s2, c2]]))
        self.m2i = _split3_lhs(np.block([[c2, -s2], [s2, c2]]))


def _dotf(a, b):
    return jnp.dot(a, b, precision=HIGHEST, preferred_element_type=F32)


def _split3_lhs(m):
    m32 = np.asarray(m, np.float32)
    hi = m32.astype(BF16)
    lo = (m32 - hi.astype(np.float32)).astype(BF16)
    return jnp.asarray(np.concatenate([hi, hi, lo], axis=-1))


def _split3_rhs(x):
    hi = x.astype(BF16)
    lo = (x - hi.astype(F32)).astype(BF16)
    return jnp.concatenate([hi, lo, hi], axis=0)


def _dot3(lhs3, x):
    return jnp.dot(lhs3, _split3_rhs(x), preferred_element_type=F32)


def _lane_cat(parts):
    return parts[0] if len(parts) == 1 else jnp.concatenate(parts, axis=1)


def _fft_stage1(plan, load_rows, f1_ref, slab_refs):
    unroll = 4

    def body(grp, carry):
        for u in range(unroll):
            n1 = grp * unroll + u
            res = _dot3(f1_ref[n1], _lane_cat([ld(n1) for ld in load_rows]))
            for p, slab in enumerate(slab_refs):
                part = res[:, p * LANES:(p + 1) * LANES]
                slab[pl.ds(n1, plan.k2p, stride=plan.srow), :] = part[:plan.k2p]
                slab[pl.ds(plan.n1 + n1, plan.k2p, stride=plan.srow), :] = part[plan.k2p:]
        return carry

    lax.fori_loop(0, plan.n1 // unroll, body, 0)


def _slab_rows(plan, slab_refs, k2):
    off = pl.multiple_of(k2 * plan.srow, SUBLANES)
    return off, _lane_cat([s[pl.ds(off, 2 * plan.n1), :] for s in slab_refs])


def _hyena_filter_kernel(z_ref, w1_ref, b1_ref, fr_ref, w2_ref, b2_ref, w3f_ref, w3b_ref, dl_ref,
                         f1_ref, m2_ref, h_ref, srcf_ref, srcb_ref, slabf_ref, slabb_ref, *, plan):
    seq = plan.seq
    fr = fr_ref[0]
    hid = jnp.sin(fr * (_dotf(z_ref[...], w1_ref[0]) + b1_ref[0]))
    hid = jnp.sin(fr * (_dotf(hid, w2_ref[0]) + b2_ref[0]))
    row = lax.broadcasted_iota(jnp.int32, (seq, 1), 0)
    t = row.astype(F32) * (1.0 / (seq - 1))
    decay = jnp.exp(-t * jnp.abs(dl_ref[...]))
    srcf_ref[...] = _dotf(hid, w3f_ref[0]) * decay
    srcb_ref[...] = jnp.where(row == 0, 0.0, _dotf(hid, w3b_ref[0]) * decay)
    slabs = (slabf_ref, slabb_ref)
    _fft_stage1(plan, [lambda n1, r=r: r[pl.ds(n1, plan.nh, stride=plan.n1), :] for r in (srcf_ref, srcb_ref)],
                f1_ref, slabs)

    unroll = 5 if plan.k2 % 5 == 0 else 1

    def body(grp, carry):
        for u in range(unroll):
            k2 = grp * unroll + u
            _, a = _slab_rows(plan, slabs, k2)
            x = _dot3(m2_ref[...], a)
            h_ref[0, k2, :plan.n1, :] = x[:plan.n1, :LANES] + x[:plan.n1, LANES:]
            h_ref[0, k2, plan.n1:, :] = x[plan.n1:, :LANES] - x[plan.n1:, LANES:]
        return carry

    lax.fori_loop(0, plan.k2 // unroll, body, 0)


def _hyena_filter_spectrum(plan, z, w1p, b1, freq, w2, b2, w3, deltas):
    depth = w3.shape[0]
    cb = HY_CB
    ncb = HYENA_WIDTH // cb
    seq = plan.seq
    lay = lambda shp: pl.BlockSpec((1,) + shp, lambda l, j: (l,) + (0,) * len(shp))
    return pl.pallas_call(
        functools.partial(_hyena_filter_kernel, plan=plan),
        grid=(depth, ncb),
        in_specs=[
            pl.BlockSpec(z.shape, lambda l, j: (0, 0)),
            lay(w1p.shape[1:]), lay(b1.shape[1:]), lay(freq.shape[1:]), lay(w2.shape[1:]), lay(b2.shape[1:]),
            pl.BlockSpec((1, FILTER_ORDER, cb), lambda l, j: (l, 0, j)),
            pl.BlockSpec((1, FILTER_ORDER, cb), lambda l, j: (l, 0, ncb + j)),
            pl.BlockSpec((1, cb), lambda l, j: (0, j)),
            pl.BlockSpec(plan.f1.shape, lambda l, j: (0, 0, 0)),
            pl.BlockSpec(plan.m2.shape, lambda l, j: (0, 0)),
        ],
        out_specs=pl.BlockSpec((1, plan.k2, 2 * plan.n1, cb), lambda l, j: (l, 0, 0, j)),
        out_shape=jax.ShapeDtypeStruct((depth, plan.k2, 2 * plan.n1, HYENA_WIDTH), F32),
        scratch_shapes=[pltpu.VMEM((seq, cb), F32), pltpu.VMEM((seq, cb), F32),
                        pltpu.VMEM((plan.k2p * plan.srow, cb), F32), pltpu.VMEM((plan.k2p * plan.srow, cb), F32)],
        compiler_params=_cparams(("arbitrary", "arbitrary"), VMEM_LIMIT),
        name="hyena_filter",
    )(z, w1p, b1, freq, w2, b2, w3, w3, deltas, plan.f1, plan.m2)


def _short_conv(u, w, b, seq):
    row = lax.broadcasted_iota(jnp.int32, (seq, 1), 0)
    prev = jnp.where(row >= 1, pltpu.roll(u, 1, 0), 0.0)
    nxt = jnp.where(row < seq - 1, pltpu.roll(u, seq - 1, 0), 0.0)
    return prev * w[0:1] + u * w[1:2] + nxt * w[2:3] + b


def _hyena_pre_kernel(x0_ref, x1_ref, v_ref, w0_ref, w1_ref, wv_ref, b0_ref, b1_ref, bv_ref, src_ref, gate_ref, *, seq):
    gate_ref[0] = _short_conv(x0_ref[0], w0_ref[...], b0_ref[...], seq)
    src_ref[0] = (_short_conv(v_ref[0], wv_ref[...], bv_ref[...], seq)
                  * _short_conv(x1_ref[0], w1_ref[...], b1_ref[...], seq))


def _hyena_pre(proj3, conv_w, conv_b):
    b, seq, _ = proj3.shape
    cb = HY_CB
    ncb = HYENA_WIDTH // cb
    c0 = OFF_HY // cb

    def u_spec(part):
        return pl.BlockSpec((1, seq, cb), lambda bi, j: (bi, 0, c0 + part * ncb + j))

    def w_spec(part):
        return pl.BlockSpec((SHORT_CONV, cb), lambda bi, j: (0, part * ncb + j))

    def b_spec(part):
        return pl.BlockSpec((1, cb), lambda bi, j: (0, part * ncb + j))

    out = pl.BlockSpec((1, seq, cb), lambda bi, j: (bi, 0, j))
    shp = jax.ShapeDtypeStruct((b, seq, HYENA_WIDTH), F32)
    return pl.pallas_call(
        functools.partial(_hyena_pre_kernel, seq=seq),
        grid=(b, ncb),
        in_specs=[u_spec(0), u_spec(1), u_spec(2), w_spec(0), w_spec(1), w_spec(2), b_spec(0), b_spec(1), b_spec(2)],
        out_specs=[out, out],
        out_shape=[shp, shp],
        compiler_params=_cparams(("arbitrary", "arbitrary"), VMEM_LIMIT),
        name="hyena_pre",
    )(proj3, proj3, proj3, conv_w, conv_w, conv_w, conv_b, conv_b, conv_b)


def _hyena_kernel(src_ref, gate_ref, skip_ref, h_ref, f1_ref, m2_ref, m2i_ref, g_ref, o_ref, *slab_refs, plan):
    npar = len(slab_refs)
    n1c = plan.n1
    _fft_stage1(plan, [lambda n1, p=p: src_ref[p, pl.ds(n1, plan.nh, stride=n1c), :] for p in range(npar)],
                f1_ref, slab_refs)

    unroll2 = 5 if plan.k2 % 5 == 0 else 1

    def stage2(grp, carry):
        for u in range(unroll2):
            k2 = grp * unroll2 + u
            off, a = _slab_rows(plan, slab_refs, k2)
            x = _dot3(m2_ref[...], a)
            h = _lane_cat([h_ref[0, k2]] * npar)
            xr, xi = x[:n1c], x[n1c:]
            hr, hi = h[:n1c], h[n1c:]
            y = jnp.concatenate([xr * hr - xi * hi, xr * hi + xi * hr], axis=0)
            back = _dot3(m2i_ref[...], y)
            for p, slab in enumerate(slab_refs):
                slab[pl.ds(off, 2 * n1c), :] = back[:, p * LANES:(p + 1) * LANES]
        return carry

    lax.fori_loop(0, plan.k2 // unroll2, stage2, 0)

    unroll3 = 4

    def stage3(grp, carry):
        for u in range(unroll3):
            n1 = grp * unroll3 + u
            a = _lane_cat([jnp.concatenate([s[pl.ds(n1, plan.k2p, stride=plan.srow), :],
                                            s[pl.ds(n1c + n1, plan.k2p, stride=plan.srow), :]], axis=0)
                           for s in slab_refs])
            y = _dot3(g_ref[n1], a)
            for p in range(npar):
                o_ref[p, pl.ds(n1, plan.nh, stride=n1c), :] = y[:, p * LANES:(p + 1) * LANES]
        return carry

    lax.fori_loop(0, n1c // unroll3, stage3, 0)
    for p in range(npar):
        o_ref[p] = (o_ref[p] + src_ref[p] * skip_ref[...]) * gate_ref[p]


def _hyena(plan, src, gate, skip, hspec, layer):
    b, seq, _ = src.shape
    cb = HY_CB
    ncb = HYENA_WIDTH // cb
    npar = 2 if b % 2 == 0 else 1
    once = pl.Buffered(1)
    const = lambda a: pl.BlockSpec(a.shape, lambda j, bi: (0,) * a.ndim, pipeline_mode=once)
    seq_spec = pl.BlockSpec((npar, seq, cb), lambda j, bi: (bi, 0, j))
    return pl.pallas_call(
        functools.partial(_hyena_kernel, plan=plan),
        grid=(ncb, b // npar),
        in_specs=[
            seq_spec, seq_spec,
            pl.BlockSpec((1, cb), lambda j, bi: (0, j)),
            pl.BlockSpec((1, plan.k2, 2 * plan.n1, cb), lambda j, bi: (layer, 0, 0, j), pipeline_mode=once),
            const(plan.f1), const(plan.m2), const(plan.m2i), const(plan.g),
        ],
        out_specs=seq_spec,
        out_shape=jax.ShapeDtypeStruct((b, seq, HYENA_WIDTH), F32),
        scratch_shapes=[pltpu.VMEM((plan.k2p * plan.srow, cb), F32)] * npar,
        compiler_params=_cparams(("arbitrary", "arbitrary"), VMEM_LIMIT),
        name="hyena",
    )(src, gate, skip, hspec, plan.f1, plan.m2, plan.m2i, plan.g)


def _pool_kernel(u_ref, w_ref, sc_ref, o_ref, *, seq):
    row = lax.broadcasted_iota(jnp.int32, (seq, 1), 0)

    def back(x, s):
        return jnp.where(row >= s, pltpu.roll(x, s, 0), 0.0)

    def fwd(x, s):
        return jnp.where(row < seq - s, pltpu.roll(x, seq - s, 0), 0.0)

    for gi, win in enumerate(POOL_WINDOWS):
        half = win // 2
        u = u_ref[0, :, gi * POOL_GROUP:(gi + 1) * POOL_GROUP]
        ahead, behind, s = u, u, 1
        while s < half:
            ahead = ahead + fwd(ahead, s)
            behind = behind + back(behind, s)
            s *= 2
        total = ahead + back(behind, 1)
        cnt = (jnp.minimum(row + half, seq) - jnp.maximum(row - half, 0)).astype(F32)
        diff = total / cnt - u
        y = jnp.dot(diff.astype(BF16), w_ref[gi], preferred_element_type=F32)
        o_ref[0, :, gi * POOL_GROUP:(gi + 1) * POOL_GROUP] = y * sc_ref[:, gi * POOL_GROUP:(gi + 1) * POOL_GROUP]


def _pool(proj3, pool_w, pool_scale):
    b, seq, _ = proj3.shape
    return pl.pallas_call(
        functools.partial(_pool_kernel, seq=seq),
        grid=(b,),
        in_specs=[
            pl.BlockSpec((1, seq, POOL_WIDTH), lambda bi: (bi, 0, OFF_POOL // POOL_WIDTH)),
            pl.BlockSpec(pool_w.shape, lambda bi: (0, 0, 0)),
            pl.BlockSpec((1, POOL_WIDTH), lambda bi: (0, 0)),
        ],
        out_specs=pl.BlockSpec((1, seq, POOL_WIDTH), lambda bi: (bi, 0, 0)),
        out_shape=jax.ShapeDtypeStruct((b, seq, POOL_WIDTH), F32),
        compiler_params=_cparams(("arbitrary",), VMEM_LIMIT),
        name="pool",
    )(proj3, pool_w, pool_scale)


def _post_mixer_kernel(ya_ref, yh_ref, yp_ref, x_ref, gn_ref, w_ref, g_ref, b_ref, o_ref, opk_ref):
    parts = []
    for src in (ya_ref, yh_ref, yp_ref):
        for j in range(src.shape[1] // OUT_GROUP):
            c = src[:, j * OUT_GROUP:(j + 1) * OUT_GROUP]
            parts.append(c * lax.rsqrt(jnp.mean(c * c, axis=-1, keepdims=True) + RMS_EPS))
    yn = (jnp.concatenate(parts, axis=-1) * gn_ref[...]).astype(BF16)
    mix = jnp.dot(yn, w_ref[...], preferred_element_type=F32)
    x1 = _ln_rows(ALPHA * x_ref[...] + mix, g_ref[...], b_ref[...])
    o_ref[...] = x1
    opk_ref[...] = _pack_bf16_pair(x1)


def _post_mixer(ya, yh, yp, x, gn, w_out, g, b):
    t, d = x.shape
    tm = _row_tile(t, 512)
    row = lambda w: pl.BlockSpec((tm, w), lambda i: (i, 0))
    vec = pl.BlockSpec((1, d), lambda i: (0, 0))
    return pl.pallas_call(
        _post_mixer_kernel,
        grid=(t // tm,),
        in_specs=[row(ATTN_WIDTH), row(HYENA_WIDTH), row(POOL_WIDTH), row(d), vec,
                  pl.BlockSpec((d, d), lambda i: (0, 0)), vec, vec],
        out_specs=[row(d), row(d // 2)],
        out_shape=[jax.ShapeDtypeStruct((t, d), F32), jax.ShapeDtypeStruct((t, d // 2), jnp.uint32)],
        compiler_params=_cparams(("arbitrary",), VMEM_LIMIT),
        name="post_mixer",
    )(ya, yh, yp, x, gn, w_out, g, b)


def _router_kernel(x_ref, rw_ref, bias_ref, tri_ref, eidx_ref, gate_ref, rank_ref, cnt_ref, carry_ref):
    tm = x_ref.shape[0]

    @pl.when(pl.program_id(0) == 0)
    def _():
        carry_ref[...] = jnp.zeros_like(carry_ref)

    logits = lax.dot_general(rw_ref[...], x_ref[...], (((1,), (1,)), ((), ())),
                             precision=HIGHEST, preferred_element_type=F32)
    scores = jax.nn.sigmoid(logits)
    choice = scores + bias_ref[...]
    sub = lax.broadcasted_iota(jnp.int32, (EXPERTS_PER_GROUP, tm), 0)
    far = jnp.int32(N_EXPERTS)

    def first_argmax(v):
        m = jnp.max(v, axis=0, keepdims=True)
        return m, jnp.min(jnp.where(v == m, sub, far), axis=0, keepdims=True)

    tiles, stiles, gscore = [], [], []
    for g in range(N_EXPERT_GROUPS):
        c = choice[g * EXPERTS_PER_GROUP:(g + 1) * EXPERTS_PER_GROUP]
        tiles.append(c)
        stiles.append(scores[g * EXPERTS_PER_GROUP:(g + 1) * EXPERTS_PER_GROUP])
        m1, i1 = first_argmax(c)
        m2 = jnp.max(jnp.where(sub == i1, REMOVED, c), axis=0, keepdims=True)
        gscore.append(m1 + m2)
    cur = jnp.concatenate(gscore, axis=0)
    gsel = jnp.zeros(cur.shape, F32)
    for _ in range(TOPK_GROUPS):
        _, ig = first_argmax(cur)
        hit = sub == ig
        gsel = jnp.where(hit, 1.0, gsel)
        cur = jnp.where(hit, REMOVED, cur)

    eids = [sub + g * EXPERTS_PER_GROUP for g in range(N_EXPERT_GROUPS)]
    masked = [jnp.where(gsel[g:g + 1] > 0.5, tiles[g], NEG_BIG) for g in range(N_EXPERT_GROUPS)]
    picked = [jnp.zeros((EXPERTS_PER_GROUP, tm), F32) for _ in range(N_EXPERT_GROUPS)]
    idxs, sels = [], []
    for _ in range(TOP_K):
        m = functools.reduce(jnp.maximum, [jnp.max(v, axis=0, keepdims=True) for v in masked])
        idx = functools.reduce(jnp.minimum, [jnp.min(jnp.where(v == m, e, far), axis=0, keepdims=True)
                                             for v, e in zip(masked, eids)])
        sc = jnp.zeros((1, tm), F32)
        for g in range(N_EXPERT_GROUPS):
            hit = eids[g] == idx
            sc = sc + jnp.sum(jnp.where(hit, stiles[g], 0.0), axis=0, keepdims=True)
            masked[g] = jnp.where(hit, REMOVED, masked[g])
            picked[g] = jnp.where(hit, 1.0, picked[g])
        idxs.append(idx)
        sels.append(sc)
    total = functools.reduce(lambda a, c: a + c, sels)
    for k in range(TOP_K):
        eidx_ref[k:k + 1, :] = idxs[k]
        gate_ref[k:k + 1, :] = sels[k] / total * ROUTED_SCALE

    sel = jnp.concatenate(picked, axis=0)
    before = jnp.dot(sel.astype(BF16), tri_ref[...], preferred_element_type=F32) + carry_ref[:, 0:1]
    for k in range(TOP_K):
        r = jnp.zeros((1, tm), F32)
        for g in range(N_EXPERT_GROUPS):
            r = r + jnp.sum(jnp.where(eids[g] == idxs[k], before[g * EXPERTS_PER_GROUP:(g + 1) * EXPERTS_PER_GROUP], 0.0),
                            axis=0, keepdims=True)
        rank_ref[k:k + 1, :] = r.astype(jnp.int32)
    carry_ref[...] = carry_ref[...] + jnp.sum(sel, axis=1, keepdims=True)
    cnt_ref[...] = carry_ref[...]


def _router(x1, rw_t, bias):
    t, d = x1.shape
    tm = _row_tile(t, 512)
    tri = jnp.asarray(np.triu(np.ones((tm, tm)), 1), BF16)
    bias_b = jnp.broadcast_to(bias[:, None], (N_EXPERTS, tm)).astype(F32)
    kt = lambda dt: jax.ShapeDtypeStruct((TOP_K, t), dt)
    return pl.pallas_call(
        _router_kernel,
        grid=(t // tm,),
        in_specs=[
            pl.BlockSpec((tm, d), lambda i: (i, 0)),
            pl.BlockSpec((N_EXPERTS, d), lambda i: (0, 0)),
            pl.BlockSpec((N_EXPERTS, tm), lambda i: (0, 0)),
            pl.BlockSpec((tm, tm), lambda i: (0, 0)),
        ],
        out_specs=[pl.BlockSpec((TOP_K, tm), lambda i: (0, i))] * 3 + [pl.BlockSpec((N_EXPERTS, LANES), lambda i: (0, 0))],
        out_shape=[kt(jnp.int32), kt(F32), kt(jnp.int32), jax.ShapeDtypeStruct((N_EXPERTS, LANES), F32)],
        scratch_shapes=[pltpu.VMEM((N_EXPERTS, LANES), F32)],
        compiler_params=_cparams(("arbitrary",)),
        name="router",
    )(x1, rw_t, bias_b, tri)


def _dispatch_plan(eidx_t, rank_t, counts, n_blocks):
    mb = MOE_MB
    cnt = counts[:, 0].astype(jnp.int32)
    padded = (cnt + mb - 1) // mb * mb
    pend = jnp.cumsum(padded)
    pstart = pend - padded
    experts = jnp.arange(N_EXPERTS, dtype=jnp.int32)
    slot_t = rank_t + jnp.sum(jnp.where(eidx_t[:, :, None] == experts, pstart, 0), axis=-1)
    block_end = pend // mb
    blocks = jnp.arange(n_blocks, dtype=jnp.int32)
    block_expert = jnp.minimum(jnp.sum((block_end[None, :] <= blocks[:, None]).astype(jnp.int32), axis=1),
                               N_EXPERTS - 1).astype(jnp.int32)
    n_used = block_end[-1:].astype(jnp.int32)
    last_block = (block_end - 1).astype(jnp.int32)
    has_rows = (padded > 0).astype(jnp.int32)
    return slot_t.astype(jnp.int32), block_expert, n_used, last_block, has_rows


def _dispatch_kernel(last_ref, has_ref, nused_ref, slot_ref, x_ref, xs_ref, zero_ref, zsem, sem, *, n_blocks):
    mb = MOE_MB
    tm = x_ref.shape[0]

    def zero_copy(block):
        return pltpu.make_async_copy(zero_ref, xs_ref.at[pl.ds(pl.multiple_of(block * mb, mb), mb)], zsem)

    @pl.when(pl.program_id(0) == 0)
    def _():
        zero_ref[...] = jnp.zeros_like(zero_ref)

        def start_e(e, c):
            @pl.when(has_ref[e] > 0)
            def _():
                zero_copy(last_ref[e]).start()
            return c

        def wait_e(e, c):
            @pl.when(has_ref[e] > 0)
            def _():
                zero_copy(last_ref[e]).wait()
            return c

        def start_b(blk, c):
            zero_copy(blk).start()
            return c

        def wait_b(blk, c):
            zero_copy(blk).wait()
            return c

        lax.fori_loop(0, N_EXPERTS, start_e, 0)
        lax.fori_loop(nused_ref[0], n_blocks, start_b, 0)
        lax.fori_loop(0, N_EXPERTS, wait_e, 0)
        lax.fori_loop(nused_ref[0], n_blocks, wait_b, 0)

    def issue(grp, c):
        base = pl.multiple_of(grp * SUBLANES, SUBLANES)
        for j in range(SUBLANES):
            for k in range(TOP_K):
                slot = slot_ref[0, (base + j) * TOP_K + k]
                pltpu.make_async_copy(x_ref.at[pl.ds(base + j, 1)], xs_ref.at[pl.ds(slot, 1)], sem).start(priority=k % 2)
        return c

    lax.fori_loop(0, tm // SUBLANES, issue, 0)
    for k in range(TOP_K):
        pltpu.make_async_copy(x_ref, xs_ref.at[pl.ds(0, tm)], sem).wait()


def _slot_tiles(slot_t, tm):
    t = slot_t.shape[1]
    return slot_t.T.reshape(t // tm, 1, tm * TOP_K)


def _dispatch(xpk, slot_t, n_used, last_block, has_rows, n_blocks):
    t, hw = xpk.shape
    tm = _row_tile(t, 256)
    grid_spec = pltpu.PrefetchScalarGridSpec(
        num_scalar_prefetch=3,
        grid=(t // tm,),
        in_specs=[
            pl.BlockSpec((None, 1, tm * TOP_K), lambda i, *_: (i, 0, 0), memory_space=pltpu.SMEM),
            pl.BlockSpec((tm, hw), lambda i, *_: (i, 0)),
        ],
        out_specs=pl.BlockSpec(memory_space=pl.ANY),
        scratch_shapes=[pltpu.VMEM((MOE_MB, hw), jnp.uint32), pltpu.SemaphoreType.DMA, pltpu.SemaphoreType.DMA],
    )
    return pl.pallas_call(
        functools.partial(_dispatch_kernel, n_blocks=n_blocks),
        grid_spec=grid_spec,
        out_shape=jax.ShapeDtypeStruct((n_blocks * MOE_MB, hw), jnp.uint32),
        compiler_params=_cparams(("arbitrary",)),
        name="dispatch",
    )(last_block, has_rows, n_used, _slot_tiles(slot_t, tm), xpk)


def _swiglu_packed(pk, wg, wu, wd):
    a, b = _unpack_bf16_pair(pk)
    gate = jnp.dot(a, wg[:HALF_D], preferred_element_type=F32) + jnp.dot(b, wg[HALF_D:], preferred_element_type=F32)
    up = jnp.dot(a, wu[:HALF_D], preferred_element_type=F32) + jnp.dot(b, wu[HALF_D:], preferred_element_type=F32)
    hid = (gate * jax.nn.sigmoid(gate) * up).astype(BF16)
    return jnp.dot(hid, wd, preferred_element_type=F32)


def _experts_kernel(be_ref, nused_ref, xs_ref, wg_ref, wu_ref, wd_ref, y_ref, wgb_ref, wub_ref, wdb_ref):
    i = pl.program_id(0)

    @pl.when((i == 0) | (be_ref[i] != be_ref[jnp.maximum(i - 1, 0)]))
    def _():
        wgb_ref[...] = wg_ref[0, 0].astype(BF16)
        wub_ref[...] = wu_ref[0, 0].astype(BF16)
        wdb_ref[...] = wd_ref[0, 0].astype(BF16)

    @pl.when(i < nused_ref[0])
    def _():
        y_ref[...] = _swiglu_packed(xs_ref[...], wgb_ref, wub_ref, wdb_ref[...])

    @pl.when(i >= nused_ref[0])
    def _():
        y_ref[...] = jnp.zeros_like(y_ref)


def _experts(xs, block_expert, n_used, wg, wu, wd, layer, n_blocks):
    hw = xs.shape[1]
    d = wd.shape[3]
    wspec = lambda w: pl.BlockSpec((1, 1) + w.shape[2:], lambda i, be, nu: (layer, be[i], 0, 0))
    grid_spec = pltpu.PrefetchScalarGridSpec(
        num_scalar_prefetch=2,
        grid=(n_blocks,),
        in_specs=[pl.BlockSpec((MOE_MB, hw), lambda i, be, nu: (i, 0)), wspec(wg), wspec(wu), wspec(wd)],
        out_specs=pl.BlockSpec((MOE_MB, d), lambda i, be, nu: (i, 0)),
        scratch_shapes=[pltpu.VMEM(wg.shape[2:], BF16), pltpu.VMEM(wu.shape[2:], BF16), pltpu.VMEM(wd.shape[2:], BF16)],
    )
    return pl.pallas_call(
        _experts_kernel,
        grid_spec=grid_spec,
        out_shape=jax.ShapeDtypeStruct((n_blocks * MOE_MB, d), F32),
        compiler_params=_cparams(("arbitrary",), VMEM_LIMIT),
        name="experts",
    )(block_expert, n_used, xs, wg, wu, wd)


def _combine_kernel(slot_ref, gates_ref, x_ref, xpk_ref, ys_ref, wg_ref, wu_ref, wd_ref, g_ref, b_ref,
                    *rest, n_prompt_blocks, final):
    outs, bufs, sem = rest[:2], rest[2:2 + TOP_K], rest[2 + TOP_K]
    tm, d = x_ref.shape
    ngroups = tm // SUBLANES

    def issue(grp, c):
        for j in range(SUBLANES):
            for k in range(TOP_K):
                slot = slot_ref[0, (grp * SUBLANES + j) * TOP_K + k]
                pltpu.make_async_copy(ys_ref.at[pl.ds(slot, 1)], bufs[k].at[grp, pl.ds(j, 1)], sem).start(priority=k % 2)
        return c

    lax.fori_loop(0, ngroups, issue, 0)
    shared = _swiglu_packed(xpk_ref[...], wg_ref, wu_ref, wd_ref[...])

    def drain(grp, c):
        for k in range(TOP_K):
            pltpu.make_async_copy(ys_ref.at[pl.ds(0, SUBLANES)], bufs[k].at[grp], sem).wait()
        return c

    lax.fori_loop(0, ngroups, drain, 0)
    gates = gates_ref[...]
    routed = gates[:, 0:1] * bufs[0][...].reshape(tm, d)
    for k in range(1, TOP_K):
        routed = routed + gates[:, k:k + 1] * bufs[k][...].reshape(tm, d)
    y = _ln_rows(ALPHA * x_ref[...] + (routed + shared), g_ref[...], b_ref[...])
    if final:
        i = pl.program_id(0)

        @pl.when(i < n_prompt_blocks)
        def _():
            outs[0][...] = y

        @pl.when(i >= n_prompt_blocks)
        def _():
            outs[1][...] = y
    else:
        outs[0][...] = y
        outs[1][...] = y.astype(BF16)


def _combine(slot_t, gates, x1, xpk, ys, wg, wu, wd, g, b, t_prompt, final):
    t, d = x1.shape
    tm = _row_tile(math.gcd(t_prompt, t - t_prompt), 128)
    npb = t_prompt // tm
    nsb = (t - t_prompt) // tm
    vec = pl.BlockSpec((1, d), lambda i: (0, 0))
    full = lambda a: pl.BlockSpec(a.shape, lambda i: (0,) * a.ndim)
    if final:
        out_specs = [pl.BlockSpec((tm, d), lambda i: (jnp.minimum(i, npb - 1), 0)),
                     pl.BlockSpec((tm, d), lambda i: (jnp.maximum(i - npb, 0), 0))]
        out_shape = [jax.ShapeDtypeStruct((t_prompt, d), F32), jax.ShapeDtypeStruct((t - t_prompt, d), F32)]
    else:
        out_specs = [pl.BlockSpec((tm, d), lambda i: (i, 0)), pl.BlockSpec((tm, d), lambda i: (i, 0))]
        out_shape = [jax.ShapeDtypeStruct((t, d), F32), jax.ShapeDtypeStruct((t, d), BF16)]
    return pl.pallas_call(
        functools.partial(_combine_kernel, n_prompt_blocks=npb, final=final),
        grid=(npb + nsb,),
        in_specs=[
            pl.BlockSpec((None, 1, tm * TOP_K), lambda i: (i, 0, 0), memory_space=pltpu.SMEM),
            pl.BlockSpec((tm, TOP_K), lambda i: (i, 0)),
            pl.BlockSpec((tm, d), lambda i: (i, 0)),
            pl.BlockSpec((tm, d // 2), lambda i: (i, 0)),
            pl.BlockSpec(memory_space=pl.ANY),
            full(wg), full(wu), full(wd), vec, vec,
        ],
        out_specs=out_specs,
        out_shape=out_shape,
        scratch_shapes=[pltpu.VMEM((tm // SUBLANES, SUBLANES, d), F32)] * TOP_K + [pltpu.SemaphoreType.DMA],
        compiler_params=_cparams(("arbitrary",), VMEM_LIMIT),
        name="combine",
    )(_slot_tiles(slot_t, tm), gates, x1, xpk, ys, wg, wu, wd, g, b)


def _filter_features(seq):
    t = np.linspace(0.0, 1.0, seq)[:, None]
    w = 2.0 * np.pi * np.arange(seq) / seq
    bands = np.linspace(1e-4, FILTER_BANDS - 1, FILTER_BANDS)
    ang = w[:, None] * bands[None, :]
    z = np.zeros((seq, FILTER_ORDER))
    z[:, :FILTER_EMB] = np.concatenate([t, np.cos(ang), -np.sin(ang)], axis=-1)
    return jnp.asarray(z, F32)


def _decay_rates():
    max_decay = math.log(DECAY_TARGET) / SHORT_DECAY_PCT
    min_decay = math.log(DECAY_TARGET) / LONG_DECAY_PCT
    return jnp.asarray(np.linspace(min_decay, max_decay, HYENA_WIDTH)[None, :], F32)


def kernel(x_prompt, x_sample, ln_in_g, ln_in_b, w_in, attn_sink, hy_conv_w, hy_conv_b, hy_f_w1, hy_f_b1,
           hy_f_freq, hy_f_w2, hy_f_b2, hy_f_w3, hy_skip, pool_w, pool_scale, out_norm_g, w_out, ln1_g, ln1_b,
           router_w, router_bias, exp_w_gate, exp_w_up, exp_w_down, sh_w_gate, sh_w_up, sh_w_down, ln2_g, ln2_b):
    bp, seq, d = x_prompt.shape
    bs = x_sample.shape[0]
    assert x_sample.shape[1] == seq and d == D_MODEL and seq % (FFT_N1 * SUBLANES) == 0
    nbatch = bp + bs
    t_prompt = bp * seq
    t = nbatch * seq
    n_blocks = t * TOP_K // MOE_MB + N_EXPERTS
    assert (t * TOP_K) % MOE_MB == 0

    plan = _FftPlan(seq)
    ctab, stab = _rope_tables(seq)
    w1p = jnp.pad(hy_f_w1, ((0, 0), (0, FILTER_ORDER - FILTER_EMB), (0, 0)))
    hspec = _hyena_filter_spectrum(plan, _filter_features(seq), w1p, hy_f_b1[:, None, :], hy_f_freq[:, None, :],
                                   hy_f_w2, hy_f_b2[:, None, :], hy_f_w3, _decay_rates())

    row = lambda v: v.reshape(1, -1)
    x, xb = _ln_in(x_prompt.reshape(t_prompt, d), x_sample.reshape(bs * seq, d), row(ln_in_g), row(ln_in_b))
    outs = None
    for l in range(DEPTH):
        proj3 = _in_proj(xb, w_in[l].astype(BF16)).reshape(nbatch, seq, IN_WIDTH)
        ya = _attention(proj3, attn_sink[l], ctab, stab)
        hsrc, hgate = _hyena_pre(proj3, hy_conv_w[l], row(hy_conv_b[l]))
        yh = _hyena(plan, hsrc, hgate, row(hy_skip[l]), hspec, l)
        yp = _pool(proj3, pool_w[l].astype(BF16), row(pool_scale[l]))
        x1, xpk = _post_mixer(ya.reshape(t, ATTN_WIDTH), yh.reshape(t, HYENA_WIDTH), yp.reshape(t, POOL_WIDTH), x,
                              row(out_norm_g[l]), w_out[l].astype(BF16), row(ln1_g[l]), row(ln1_b[l]))
        eidx_t, gate_t, rank_t, counts = _router(x1, router_w[l].T, router_bias[l])
        slot_t, block_expert, n_used, last_block, has_rows = _dispatch_plan(eidx_t, rank_t, counts, n_blocks)
        xs = _dispatch(xpk, slot_t, n_used, last_block, has_rows, n_blocks)
        ys = _experts(xs, block_expert, n_used, exp_w_gate, exp_w_up, exp_w_down, l, n_blocks)
        res = _combine(slot_t, gate_t.T, x1, xpk, ys, sh_w_gate[l].astype(BF16), sh_w_up[l].astype(BF16),
                       sh_w_down[l].astype(BF16), row(ln2_g[l]), row(ln2_b[l]), t_prompt, final=(l == DEPTH - 1))
        if l == DEPTH - 1:
            outs = res
        else:
            x, xb = res
    return outs[0].reshape(bp, seq, d), outs[1].reshape(bs, seq, d)
```

```python
import functools
import math

import jax
import jax.numpy as jnp
import numpy as np
from jax import lax
from jax.experimental import pallas as pl
from jax.experimental.pallas import tpu as pltpu

F32 = jnp.float32
BF16 = jnp.bfloat16
HIGHEST = lax.Precision.HIGHEST

D_MODEL = 2048
DEPTH = 2
HEAD_DIM = 128
ATTN_WIDTH = D_MODEL // 2
N_Q_HEADS = ATTN_WIDTH // HEAD_DIM
N_KV_HEADS = 2
Q_PER_KV = N_Q_HEADS // N_KV_HEADS
KV_WIDTH = N_KV_HEADS * HEAD_DIM
HYENA_WIDTH = D_MODEL // 4
POOL_WIDTH = D_MODEL - ATTN_WIDTH - HYENA_WIDTH
IN_WIDTH = ATTN_WIDTH + 2 * KV_WIDTH + 3 * HYENA_WIDTH + POOL_WIDTH
OFF_K = ATTN_WIDTH
OFF_V = OFF_K + KV_WIDTH
OFF_HY = OFF_V + KV_WIDTH
OFF_POOL = OFF_HY + 3 * HYENA_WIDTH

WINDOW = 128
BLOCK = 128
ROPE_THETA = 500000.0
ROPE_DIM = HEAD_DIM // 4
ROPE_HALF = ROPE_DIM // 2
NEG_BIG = -1e30
REMOVED = -3.0e38

SHORT_CONV = 3
FILTER_EMB = 33
FILTER_BANDS = (FILTER_EMB - 1) // 2
FILTER_ORDER = 64
DECAY_TARGET = 1e-2
SHORT_DECAY_PCT = 0.3
LONG_DECAY_PCT = 1.5

POOL_WINDOWS = (2, 4, 8, 16)
POOL_GROUP = POOL_WIDTH // len(POOL_WINDOWS)
OUT_GROUP = 128

N_EXPERTS = 64
TOP_K = 8
N_EXPERT_GROUPS = 8
EXPERTS_PER_GROUP = N_EXPERTS // N_EXPERT_GROUPS
TOPK_GROUPS = 4
EXPERT_HIDDEN = 512
SHARED_HIDDEN = 512
ROUTED_SCALE = 2.5

ALPHA = (2 * DEPTH) ** 0.25
LN_EPS = 1e-5
RMS_EPS = 1e-6

SUBLANES = 8
LANES = 128
VMEM_LIMIT = 56 * 1024 * 1024

FFT_N1 = 64
HY_CB = 128
MOE_MB = 512
HALF_D = D_MODEL // 2


def _cparams(sem, vmem=None):
    return pltpu.CompilerParams(dimension_semantics=sem, vmem_limit_bytes=vmem)


def _row_tile(t, pref):
    while t % pref:
        pref //= 2
    return pref


def _ln_rows(x, g, b):
    mu = jnp.mean(x, axis=-1, keepdims=True)
    xc = x - mu
    var = jnp.mean(xc * xc, axis=-1, keepdims=True)
    return xc * lax.rsqrt(var + LN_EPS) * g + b


def _pack_bf16_pair(x):
    h = x.shape[1] // 2
    hi = lax.bitcast_convert_type(x[:, :h].astype(BF16).astype(F32), jnp.uint32)
    lo = lax.bitcast_convert_type(x[:, h:].astype(BF16).astype(F32), jnp.uint32)
    return hi | (lo >> 16)


def _unpack_bf16_pair(pk):
    a = lax.bitcast_convert_type(pk & jnp.uint32(0xFFFF0000), F32).astype(BF16)
    b = lax.bitcast_convert_type(pk << 16, F32).astype(BF16)
    return a, b


def _ln_in_kernel(xp_ref, xs_ref, g_ref, b_ref, o_ref, ob_ref, *, n_prompt_blocks):
    i = pl.program_id(0)

    def emit(x):
        y = _ln_rows(x, g_ref[...], b_ref[...])
        o_ref[...] = y
        ob_ref[...] = y.astype(BF16)

    @pl.when(i < n_prompt_blocks)
    def _():
        emit(xp_ref[...])

    @pl.when(i >= n_prompt_blocks)
    def _():
        emit(xs_ref[...])


def _ln_in(xp, xs, g, b):
    tp, d = xp.shape
    ts = xs.shape[0]
    tm = _row_tile(math.gcd(tp, ts), 512)
    npb, nsb = tp // tm, ts // tm
    t = tp + ts
    return pl.pallas_call(
        functools.partial(_ln_in_kernel, n_prompt_blocks=npb),
        grid=(npb + nsb,),
        in_specs=[
            pl.BlockSpec((tm, d), lambda i: (jnp.minimum(i, npb - 1), 0)),
            pl.BlockSpec((tm, d), lambda i: (jnp.maximum(i - npb, 0), 0)),
            pl.BlockSpec((1, d), lambda i: (0, 0)),
            pl.BlockSpec((1, d), lambda i: (0, 0)),
        ],
        out_specs=[pl.BlockSpec((tm, d), lambda i: (i, 0)), pl.BlockSpec((tm, d), lambda i: (i, 0))],
        out_shape=[jax.ShapeDtypeStruct((t, d), F32), jax.ShapeDtypeStruct((t, d), BF16)],
        compiler_params=_cparams(("arbitrary",)),
        name="ln_in",
    )(xp, xs, g, b)


def _matmul_kernel(x_ref, w_ref, o_ref):
    o_ref[...] = jnp.dot(x_ref[...], w_ref[...], preferred_element_type=F32).astype(o_ref.dtype)


def _in_proj(xb, w):
    t, k = xb.shape
    n = w.shape[1]
    tm = _row_tile(t, 1024)
    tn = 512
    return pl.pallas_call(
        _matmul_kernel,
        grid=(t // tm, n // tn),
        in_specs=[pl.BlockSpec((tm, k), lambda i, j: (i, 0)), pl.BlockSpec((k, tn), lambda i, j: (0, j))],
        out_specs=pl.BlockSpec((tm, tn), lambda i, j: (i, j)),
        out_shape=jax.ShapeDtypeStruct((t, n), BF16),
        compiler_params=_cparams(("arbitrary", "arbitrary")),
        name="in_proj",
    )(xb, w)


def _rope_tables(seq):
    pos = np.arange(-BLOCK, seq + BLOCK, dtype=np.float64)
    inv = ROPE_THETA ** (-np.arange(ROPE_HALF, dtype=np.float64) / ROPE_HALF)
    ang = pos[:, None] * inv[None, :]
    c = np.ones((pos.shape[0], HEAD_DIM))
    s = np.zeros((pos.shape[0], HEAD_DIM))
    c[:, :ROPE_HALF] = np.cos(ang)
    c[:, ROPE_HALF:ROPE_DIM] = np.cos(ang)
    s[:, :ROPE_HALF] = -np.sin(ang)
    s[:, ROPE_HALF:ROPE_DIM] = np.sin(ang)
    return jnp.asarray(c, F32), jnp.asarray(s, F32)


def _rope(x, c, s):
    lane = lax.broadcasted_iota(jnp.int32, x.shape, 1)
    partner = jnp.where(lane < ROPE_HALF, pltpu.roll(x, HEAD_DIM - ROPE_HALF, 1), pltpu.roll(x, ROPE_HALF, 1))
    return x * c + partner * s


def _attn_kernel(sink_ref, q_ref, *refs, nb, qb):
    nkb = qb + 2
    k_refs, v_refs = refs[:nkb], refs[nkb:2 * nkb]
    ct_ref, st_ref, o_ref = refs[2 * nkb:]
    i = pl.program_id(1)
    base = pl.multiple_of(i * (qb * BLOCK), BLOCK)
    cw = ct_ref[pl.ds(base, nkb * BLOCK), :]
    sw = st_ref[pl.ds(base, nkb * BLOCK), :]
    kwin = jnp.concatenate([r[0] for r in k_refs], axis=0).astype(F32)
    vwin = jnp.concatenate([r[0] for r in v_refs], axis=0)
    q = q_ref[0].astype(F32)

    rows = Q_PER_KV * BLOCK
    rr = lax.broadcasted_iota(jnp.int32, (rows, 3 * BLOCK), 0) & (BLOCK - 1)
    cc = lax.broadcasted_iota(jnp.int32, (rows, 3 * BLOCK), 1)
    rel = cc - rr
    in_window = (rel >= BLOCK - WINDOW) & (rel <= BLOCK + WINDOW)
    rgrp = lax.broadcasted_iota(jnp.int32, (rows, 1), 0) // BLOCK

    for h in range(N_KV_HEADS):
        kh_all = _rope(kwin[:, h * HEAD_DIM:(h + 1) * HEAD_DIM], cw, sw).astype(BF16)
        vh_all = vwin[:, h * HEAD_DIM:(h + 1) * HEAD_DIM].astype(BF16)
        sink = jnp.zeros((rows, 1), F32)
        for g in range(Q_PER_KV):
            sink = jnp.where(rgrp == g, sink_ref[h * Q_PER_KV + g], sink)
        for sb in range(qb):
            blk = i * qb + sb
            lo = jnp.where(blk == 0, BLOCK, 0)
            hi = jnp.where(blk == nb - 1, 2 * BLOCK, 3 * BLOCK)
            valid = in_window & (cc >= lo) & (cc < hi)
            cq, sq = cw[(sb + 1) * BLOCK:(sb + 2) * BLOCK], sw[(sb + 1) * BLOCK:(sb + 2) * BLOCK]
            kh, vh = kh_all[sb * BLOCK:(sb + 3) * BLOCK], vh_all[sb * BLOCK:(sb + 3) * BLOCK]
            qs = []
            for g in range(Q_PER_KV):
                hq = h * Q_PER_KV + g
                qs.append(_rope(q[sb * BLOCK:(sb + 1) * BLOCK, hq * HEAD_DIM:(hq + 1) * HEAD_DIM], cq, sq).astype(BF16))
            qg = jnp.concatenate(qs, axis=0)
            s = lax.dot_general(qg, kh, (((1,), (1,)), ((), ())), preferred_element_type=F32) * (HEAD_DIM ** -0.5)
            s = jnp.where(valid, s, NEG_BIG)
            m = jnp.maximum(jnp.max(s, axis=-1, keepdims=True), sink)
            p = jnp.exp(s - m)
            denom = jnp.sum(p, axis=-1, keepdims=True) + jnp.exp(sink - m)
            o = jnp.dot((p / denom).astype(BF16), vh, preferred_element_type=F32)
            for g in range(Q_PER_KV):
                hq = h * Q_PER_KV + g
                o_ref[0, sb * BLOCK:(sb + 1) * BLOCK, hq * HEAD_DIM:(hq + 1) * HEAD_DIM] = o[g * BLOCK:(g + 1) * BLOCK]


def _attention(proj3, sink, ctab, stab):
    b, seq, _ = proj3.shape
    nb = seq // BLOCK
    qb = 2 if nb % 2 == 0 else 1
    kcol, vcol = OFF_K // KV_WIDTH, OFF_V // KV_WIDTH

    def kv_spec(col, shift):
        return pl.BlockSpec((1, BLOCK, KV_WIDTH), lambda bi, i: (bi, jnp.clip(i * qb + shift, 0, nb - 1), col))

    shifts = range(-1, qb + 1)
    n_kv = 2 * len(shifts)
    return pl.pallas_call(
        functools.partial(_attn_kernel, nb=nb, qb=qb),
        grid=(b, nb // qb),
        in_specs=[pl.BlockSpec(memory_space=pltpu.SMEM),
                  pl.BlockSpec((1, qb * BLOCK, ATTN_WIDTH), lambda bi, i: (bi, i, 0))]
                 + [kv_spec(kcol, s) for s in shifts] + [kv_spec(vcol, s) for s in shifts]
                 + [pl.BlockSpec(ctab.shape, lambda bi, i: (0, 0)), pl.BlockSpec(stab.shape, lambda bi, i: (0, 0))],
        out_specs=pl.BlockSpec((1, qb * BLOCK, ATTN_WIDTH), lambda bi, i: (bi, i, 0)),
        out_shape=jax.ShapeDtypeStruct((b, seq, ATTN_WIDTH), F32),
        compiler_params=_cparams(("arbitrary", "arbitrary")),
        name="attention",
    )(sink, *([proj3] * (1 + n_kv)), ctab, stab)


class _FftPlan:
    def __init__(self, seq):
        self.seq = seq
        self.n = 2 * seq
        self.n1 = FFT_N1
        self.n2 = self.n // self.n1
        self.nh = seq // self.n1
        self.k2 = self.n2 // 2 + 1
        self.k2p = -(-self.k2 // SUBLANES) * SUBLANES
        self.srow = 2 * self.n1 + SUBLANES
        n1 = np.arange(self.n1)
        n2 = np.arange(self.nh)
        k2 = np.arange(self.k2)
        npos = self.n1 * n2[None, None, :] + n1[:, None, None]
        ang = 2.0 * np.pi * k2[None, :, None] * npos / self.n
        f1 = np.zeros((self.n1, 2 * self.k2p, self.nh))
        f1[:, :self.k2] = np.cos(ang)
        f1[:, self.k2p:self.k2p + self.k2] = -np.sin(ang)
        wk = np.full((self.k2,), 2.0)
        wk[0] = 1.0
        wk[-1] = 1.0
        gm = np.zeros((self.n1, self.nh, 2 * self.k2p))
        angt = np.transpose(ang, (0, 2, 1))
        gm[:, :, :self.k2] = np.cos(angt) * wk / self.n
        gm[:, :, self.k2p:self.k2p + self.k2] = -np.sin(angt) * wk / self.n
        a2 = 2.0 * np.pi * np.outer(n1, n1) / self.n1
        c2, s2 = np.cos(a2), np.sin(a2)
        self.f1 = _split3_lhs(f1)
        self.g = _split3_lhs(gm)
        self.m2 = _split3_lhs(np.block([[c2, s2], [-s2, c2]]))
        self.m2i = _split3_lhs(np.block([[c2, -s2], [s2, c2]]))


def _dotf(a, b):
    return jnp.dot(a, b, precision=HIGHEST, preferred_element_type=F32)


def _split3_lhs(m):
    m32 = np.asarray(m, np.float32)
    hi = m32.astype(BF16)
    lo = (m32 - hi.astype(np.float32)).astype(BF16)
    return jnp.asarray(np.concatenate([hi, hi, lo], axis=-1))


def _split3_rhs(x):
    hi = x.astype(BF16)
    lo = (x - hi.astype(F32)).astype(BF16)
    return jnp.concatenate([hi, lo, hi], axis=0)


def _dot3(lhs3, x):
    return jnp.dot(lhs3, _split3_rhs(x), preferred_element_type=F32)


def _lane_cat(parts):
    return parts[0] if len(parts) == 1 else jnp.concatenate(parts, axis=1)


def _fft_stage1(plan, load_rows, f1_ref, slab_refs):
    unroll = 4

    def body(grp, carry):
        for u in range(unroll):
            n1 = grp * unroll + u
            res = _dot3(f1_ref[n1], _lane_cat([ld(n1) for ld in load_rows]))
            for p, slab in enumerate(slab_refs):
                part = res[:, p * LANES:(p + 1) * LANES]
                slab[pl.ds(n1, plan.k2p, stride=plan.srow), :] = part[:plan.k2p]
                slab[pl.ds(plan.n1 + n1, plan.k2p, stride=plan.srow), :] = part[plan.k2p:]
        return carry

    lax.fori_loop(0, plan.n1 // unroll, body, 0)


def _slab_rows(plan, slab_refs, k2):
    off = pl.multiple_of(k2 * plan.srow, SUBLANES)
    return off, _lane_cat([s[pl.ds(off, 2 * plan.n1), :] for s in slab_refs])


def _hyena_filter_kernel(z_ref, w1_ref, b1_ref, fr_ref, w2_ref, b2_ref, w3f_ref, w3b_ref, dl_ref,
                         f1_ref, m2_ref, h_ref, srcf_ref, srcb_ref, slabf_ref, slabb_ref, *, plan):
    seq = plan.seq
    fr = fr_ref[0]
    hid = jnp.sin(fr * (_dotf(z_ref[...], w1_ref[0]) + b1_ref[0]))
    hid = jnp.sin(fr * (_dotf(hid, w2_ref[0]) + b2_ref[0]))
    row = lax.broadcasted_iota(jnp.int32, (seq, 1), 0)
    t = row.astype(F32) * (1.0 / (seq - 1))
    decay = jnp.exp(-t * jnp.abs(dl_ref[...]))
    srcf_ref[...] = _dotf(hid, w3f_ref[0]) * decay
    srcb_ref[...] = jnp.where(row == 0, 0.0, _dotf(hid, w3b_ref[0]) * decay)
    slabs = (slabf_ref, slabb_ref)
    _fft_stage1(plan, [lambda n1, r=r: r[pl.ds(n1, plan.nh, stride=plan.n1), :] for r in (srcf_ref, srcb_ref)],
                f1_ref, slabs)

    unroll = 5 if plan.k2 % 5 == 0 else 1

    def body(grp, carry):
        for u in range(unroll):
            k2 = grp * unroll + u
            _, a = _slab_rows(plan, slabs, k2)
            x = _dot3(m2_ref[...], a)
            h_ref[0, k2, :plan.n1, :] = x[:plan.n1, :LANES] + x[:plan.n1, LANES:]
            h_ref[0, k2, plan.n1:, :] = x[plan.n1:, :LANES] - x[plan.n1:, LANES:]
        return carry

    lax.fori_loop(0, plan.k2 // unroll, body, 0)


def _hyena_filter_spectrum(plan, z, w1p, b1, freq, w2, b2, w3, deltas):
    depth = w3.shape[0]
    cb = HY_CB
    ncb = HYENA_WIDTH // cb
    seq = plan.seq
    lay = lambda shp: pl.BlockSpec((1,) + shp, lambda l, j: (l,) + (0,) * len(shp))
    return pl.pallas_call(
        functools.partial(_hyena_filter_kernel, plan=plan),
        grid=(depth, ncb),
        in_specs=[
            pl.BlockSpec(z.shape, lambda l, j: (0, 0)),
            lay(w1p.shape[1:]), lay(b1.shape[1:]), lay(freq.shape[1:]), lay(w2.shape[1:]), lay(b2.shape[1:]),
            pl.BlockSpec((1, FILTER_ORDER, cb), lambda l, j: (l, 0, j)),
            pl.BlockSpec((1, FILTER_ORDER, cb), lambda l, j: (l, 0, ncb + j)),
            pl.BlockSpec((1, cb), lambda l, j: (0, j)),
            pl.BlockSpec(plan.f1.shape, lambda l, j: (0, 0, 0)),
            pl.BlockSpec(plan.m2.shape, lambda l, j: (0, 0)),
        ],
        out_specs=pl.BlockSpec((1, plan.k2, 2 * plan.n1, cb), lambda l, j: (l, 0, 0, j)),
        out_shape=jax.ShapeDtypeStruct((depth, plan.k2, 2 * plan.n1, HYENA_WIDTH), F32),
        scratch_shapes=[pltpu.VMEM((seq, cb), F32), pltpu.VMEM((seq, cb), F32),
                        pltpu.VMEM((plan.k2p * plan.srow, cb), F32), pltpu.VMEM((plan.k2p * plan.srow, cb), F32)],
        compiler_params=_cparams(("arbitrary", "arbitrary"), VMEM_LIMIT),
        name="hyena_filter",
    )(z, w1p, b1, freq, w2, b2, w3, w3, deltas, plan.f1, plan.m2)


def _short_conv(u, w, b, seq):
    row = lax.broadcasted_iota(jnp.int32, (seq, 1), 0)
    prev = jnp.where(row >= 1, pltpu.roll(u, 1, 0), 0.0)
    nxt = jnp.where(row < seq - 1, pltpu.roll(u, seq - 1, 0), 0.0)
    return prev * w[0:1] + u * w[1:2] + nxt * w[2:3] + b


def _hyena_pre_kernel(x0_ref, x1_ref, v_ref, w0_ref, w1_ref, wv_ref, b0_ref, b1_ref, bv_ref, src_ref, gate_ref, *, seq):
    gate_ref[0] = _short_conv(x0_ref[0].astype(F32), w0_ref[...], b0_ref[...], seq)
    src_ref[0] = (_short_conv(v_ref[0].astype(F32), wv_ref[...], bv_ref[...], seq)
                  * _short_conv(x1_ref[0].astype(F32), w1_ref[...], b1_ref[...], seq))


def _hyena_pre(proj3, conv_w, conv_b):
    b, seq, _ = proj3.shape
    cb = HY_CB
    ncb = HYENA_WIDTH // cb
    c0 = OFF_HY // cb

    def u_spec(part):
        return pl.BlockSpec((1, seq, cb), lambda bi, j: (bi, 0, c0 + part * ncb + j))

    def w_spec(part):
        return pl.BlockSpec((SHORT_CONV, cb), lambda bi, j: (0, part * ncb + j))

    def b_spec(part):
        return pl.BlockSpec((1, cb), lambda bi, j: (0, part * ncb + j))

    out = pl.BlockSpec((1, seq, cb), lambda bi, j: (bi, 0, j))
    shp = jax.ShapeDtypeStruct((b, seq, HYENA_WIDTH), F32)
    return pl.pallas_call(
        functools.partial(_hyena_pre_kernel, seq=seq),
        grid=(b, ncb),
        in_specs=[u_spec(0), u_spec(1), u_spec(2), w_spec(0), w_spec(1), w_spec(2), b_spec(0), b_spec(1), b_spec(2)],
        out_specs=[out, out],
        out_shape=[shp, shp],
        compiler_params=_cparams(("arbitrary", "arbitrary"), VMEM_LIMIT),
        name="hyena_pre",
    )(proj3, proj3, proj3, conv_w, conv_w, conv_w, conv_b, conv_b, conv_b)


def _hyena_kernel(src_ref, gate_ref, skip_ref, h_ref, f1_ref, m2_ref, m2i_ref, g_ref, o_ref, *slab_refs, plan):
    npar = len(slab_refs)
    n1c = plan.n1
    _fft_stage1(plan, [lambda n1, p=p: src_ref[p, pl.ds(n1, plan.nh, stride=n1c), :] for p in range(npar)],
                f1_ref, slab_refs)

    unroll2 = 5 if plan.k2 % 5 == 0 else 1

    def stage2(grp, carry):
        for u in range(unroll2):
            k2 = grp * unroll2 + u
            off, a = _slab_rows(plan, slab_refs, k2)
            x = _dot3(m2_ref[...], a)
            h = _lane_cat([h_ref[0, k2]] * npar)
            xr, xi = x[:n1c], x[n1c:]
            hr, hi = h[:n1c], h[n1c:]
            y = jnp.concatenate([xr * hr - xi * hi, xr * hi + xi * hr], axis=0)
            back = _dot3(m2i_ref[...], y)
            for p, slab in enumerate(slab_refs):
                slab[pl.ds(off, 2 * n1c), :] = back[:, p * LANES:(p + 1) * LANES]
        return carry

    lax.fori_loop(0, plan.k2 // unroll2, stage2, 0)

    unroll3 = 4

    def stage3(grp, carry):
        for u in range(unroll3):
            n1 = grp * unroll3 + u
            a = _lane_cat([jnp.concatenate([s[pl.ds(n1, plan.k2p, stride=plan.srow), :],
                                            s[pl.ds(n1c + n1, plan.k2p, stride=plan.srow), :]], axis=0)
                           for s in slab_refs])
            y = _dot3(g_ref[n1], a)
            for p in range(npar):
                o_ref[p, pl.ds(n1, plan.nh, stride=n1c), :] = y[:, p * LANES:(p + 1) * LANES]
        return carry

    lax.fori_loop(0, n1c // unroll3, stage3, 0)
    for p in range(npar):
        o_ref[p] = (o_ref[p] + src_ref[p] * skip_ref[...]) * gate_ref[p]


def _hyena(plan, src, gate, skip, hspec, layer):
    b, seq, _ = src.shape
    cb = HY_CB
    ncb = HYENA_WIDTH // cb
    npar = 2 if b % 2 == 0 else 1
    once = pl.Buffered(1)
    const = lambda a: pl.BlockSpec(a.shape, lambda j, bi: (0,) * a.ndim, pipeline_mode=once)
    seq_spec = pl.BlockSpec((npar, seq, cb), lambda j, bi: (bi, 0, j))
    return pl.pallas_call(
        functools.partial(_hyena_kernel, plan=plan),
        grid=(ncb, b // npar),
        in_specs=[
            seq_spec, seq_spec,
            pl.BlockSpec((1, cb), lambda j, bi: (0, j)),
            pl.BlockSpec((1, plan.k2, 2 * plan.n1, cb), lambda j, bi: (layer, 0, 0, j), pipeline_mode=once),
            const(plan.f1), const(plan.m2), const(plan.m2i), const(plan.g),
        ],
        out_specs=seq_spec,
        out_shape=jax.ShapeDtypeStruct((b, seq, HYENA_WIDTH), F32),
        scratch_shapes=[pltpu.VMEM((plan.k2p * plan.srow, cb), F32)] * npar,
        compiler_params=_cparams(("arbitrary", "arbitrary"), VMEM_LIMIT),
        name="hyena",
    )(src, gate, skip, hspec, plan.f1, plan.m2, plan.m2i, plan.g)


def _pool_kernel(u_ref, w_ref, sc_ref, o_ref, *, seq):
    row = lax.broadcasted_iota(jnp.int32, (seq, 1), 0)

    def back(x, s):
        return jnp.where(row >= s, pltpu.roll(x, s, 0), 0.0)

    def fwd(x, s):
        return jnp.where(row < seq - s, pltpu.roll(x, seq - s, 0), 0.0)

    for gi, win in enumerate(POOL_WINDOWS):
        half = win // 2
        u = u_ref[0, :, gi * POOL_GROUP:(gi + 1) * POOL_GROUP].astype(F32)
        ahead, behind, s = u, u, 1
        while s < half:
            ahead = ahead + fwd(ahead, s)
            behind = behind + back(behind, s)
            s *= 2
        total = ahead + back(behind, 1)
        cnt = (jnp.minimum(row + half, seq) - jnp.maximum(row - half, 0)).astype(F32)
        diff = total / cnt - u
        y = jnp.dot(diff.astype(BF16), w_ref[gi], preferred_element_type=F32)
        o_ref[0, :, gi * POOL_GROUP:(gi + 1) * POOL_GROUP] = y * sc_ref[:, gi * POOL_GROUP:(gi + 1) * POOL_GROUP]


def _pool(proj3, pool_w, pool_scale):
    b, seq, _ = proj3.shape
    return pl.pallas_call(
        functools.partial(_pool_kernel, seq=seq),
        grid=(b,),
        in_specs=[
            pl.BlockSpec((1, seq, POOL_WIDTH), lambda bi: (bi, 0, OFF_POOL // POOL_WIDTH)),
            pl.BlockSpec(pool_w.shape, lambda bi: (0, 0, 0)),
            pl.BlockSpec((1, POOL_WIDTH), lambda bi: (0, 0)),
        ],
        out_specs=pl.BlockSpec((1, seq, POOL_WIDTH), lambda bi: (bi, 0, 0)),
        out_shape=jax.ShapeDtypeStruct((b, seq, POOL_WIDTH), F32),
        compiler_params=_cparams(("arbitrary",), VMEM_LIMIT),
        name="pool",
    )(proj3, pool_w, pool_scale)


def _post_mixer_kernel(ya_ref, yh_ref, yp_ref, x_ref, gn_ref, w_ref, g_ref, b_ref, rw_ref, bias_ref, tri_ref,
                       o_ref, opk_ref, eidx_ref, gate_ref, rank_ref, cnt_ref, carry_ref):
    parts = []
    for src in (ya_ref, yh_ref, yp_ref):
        for j in range(src.shape[1] // OUT_GROUP):
            c = src[:, j * OUT_GROUP:(j + 1) * OUT_GROUP]
            parts.append(c * lax.rsqrt(jnp.mean(c * c, axis=-1, keepdims=True) + RMS_EPS))
    yn = (jnp.concatenate(parts, axis=-1) * gn_ref[...]).astype(BF16)
    mix = jnp.dot(yn, w_ref[...], preferred_element_type=F32)
    x1 = _ln_rows(ALPHA * x_ref[...] + mix, g_ref[...], b_ref[...])
    o_ref[...] = x1
    opk_ref[...] = _pack_bf16_pair(x1)
    _route(x1, rw_ref, bias_ref, tri_ref, eidx_ref, gate_ref, rank_ref, cnt_ref, carry_ref)


def _post_mixer(ya, yh, yp, x, gn, w_out, g, b, router_w, router_bias):
    t, d = x.shape
    tm = _row_tile(t, 512)
    row = lambda w: pl.BlockSpec((tm, w), lambda i: (i, 0))
    vec = pl.BlockSpec((1, d), lambda i: (0, 0))
    once = pl.Buffered(1)
    const = lambda a: pl.BlockSpec(a.shape, lambda i: (0,) * a.ndim, pipeline_mode=once)
    rw3 = _split3_lhs_traced(router_w.T)
    bias_b = jnp.broadcast_to(router_bias[:, None], (N_EXPERTS, tm)).astype(F32)
    tri = jnp.asarray(np.triu(np.ones((tm, tm)), 1), BF16)
    kt = lambda dt: jax.ShapeDtypeStruct((TOP_K, t), dt)
    kspec = pl.BlockSpec((TOP_K, tm), lambda i: (0, i))
    return pl.pallas_call(
        _post_mixer_kernel,
        grid=(t // tm,),
        in_specs=[row(ATTN_WIDTH), row(HYENA_WIDTH), row(POOL_WIDTH), row(d), vec, const(w_out), vec, vec,
                  const(rw3), const(bias_b), const(tri)],
        out_specs=[row(d), row(d // 2), kspec, kspec, kspec, pl.BlockSpec((N_EXPERTS, LANES), lambda i: (0, 0))],
        out_shape=[jax.ShapeDtypeStruct((t, d), F32), jax.ShapeDtypeStruct((t, d // 2), jnp.uint32),
                   kt(jnp.int32), kt(F32), kt(jnp.int32), jax.ShapeDtypeStruct((N_EXPERTS, LANES), F32)],
        scratch_shapes=[pltpu.VMEM((N_EXPERTS, LANES), F32)],
        compiler_params=_cparams(("arbitrary",), VMEM_LIMIT),
        name="post_mixer",
    )(ya, yh, yp, x, gn, w_out, g, b, rw3, bias_b, tri)


def _split3_lhs_traced(m):
    hi = m.astype(BF16)
    lo = (m - hi.astype(F32)).astype(BF16)
    return jnp.concatenate([hi, hi, lo], axis=-1)


def _route(x1, rw_ref, bias_ref, tri_ref, eidx_ref, gate_ref, rank_ref, cnt_ref, carry_ref):
    tm = x1.shape[0]

    @pl.when(pl.program_id(0) == 0)
    def _():
        carry_ref[...] = jnp.zeros_like(carry_ref)

    xh = x1.astype(BF16)
    xl = (x1 - xh.astype(F32)).astype(BF16)
    logits = lax.dot_general(rw_ref[...], jnp.concatenate([xh, xl, xh], axis=1), (((1,), (1,)), ((), ())),
                             preferred_element_type=F32)
    scores = jax.nn.sigmoid(logits)
    choice = scores + bias_ref[...]
    sub = lax.broadcasted_iota(jnp.int32, (EXPERTS_PER_GROUP, tm), 0)
    far = jnp.int32(N_EXPERTS)

    def first_argmax(v):
        m = jnp.max(v, axis=0, keepdims=True)
        return m, jnp.min(jnp.where(v == m, sub, far), axis=0, keepdims=True)

    tiles, stiles, gscore = [], [], []
    for g in range(N_EXPERT_GROUPS):
        c = choice[g * EXPERTS_PER_GROUP:(g + 1) * EXPERTS_PER_GROUP]
        tiles.append(c)
        stiles.append(scores[g * EXPERTS_PER_GROUP:(g + 1) * EXPERTS_PER_GROUP])
        m1, i1 = first_argmax(c)
        m2 = jnp.max(jnp.where(sub == i1, REMOVED, c), axis=0, keepdims=True)
        gscore.append(m1 + m2)
    cur = jnp.concatenate(gscore, axis=0)
    gsel = jnp.zeros(cur.shape, F32)
    for _ in range(TOPK_GROUPS):
        _, ig = first_argmax(cur)
        hit = sub == ig
        gsel = jnp.where(hit, 1.0, gsel)
        cur = jnp.where(hit, REMOVED, cur)

    eids = [sub + g * EXPERTS_PER_GROUP for g in range(N_EXPERT_GROUPS)]
    masked = [jnp.where(gsel[g:g + 1] > 0.5, tiles[g], NEG_BIG) for g in range(N_EXPERT_GROUPS)]
    picked = [jnp.zeros((EXPERTS_PER_GROUP, tm), F32) for _ in range(N_EXPERT_GROUPS)]
    idxs, sels = [], []
    for _ in range(TOP_K):
        m = functools.reduce(jnp.maximum, [jnp.max(v, axis=0, keepdims=True) for v in masked])
        idx = functools.reduce(jnp.minimum, [jnp.min(jnp.where(v == m, e, far), axis=0, keepdims=True)
                                             for v, e in zip(masked, eids)])
        sc = jnp.zeros((1, tm), F32)
        for g in range(N_EXPERT_GROUPS):
            hit = eids[g] == idx
            sc = sc + jnp.sum(jnp.where(hit, stiles[g], 0.0), axis=0, keepdims=True)
            masked[g] = jnp.where(hit, REMOVED, masked[g])
            picked[g] = jnp.where(hit, 1.0, picked[g])
        idxs.append(idx)
        sels.append(sc)
    total = functools.reduce(lambda a, c: a + c, sels)
    for k in range(TOP_K):
        eidx_ref[k:k + 1, :] = idxs[k]
        gate_ref[k:k + 1, :] = sels[k] / total * ROUTED_SCALE

    sel = jnp.concatenate(picked, axis=0)
    before = jnp.dot(sel.astype(BF16), tri_ref[...], preferred_element_type=F32) + carry_ref[:, 0:1]
    for k in range(TOP_K):
        r = jnp.zeros((1, tm), F32)
        for g in range(N_EXPERT_GROUPS):
            r = r + jnp.sum(jnp.where(eids[g] == idxs[k], before[g * EXPERTS_PER_GROUP:(g + 1) * EXPERTS_PER_GROUP], 0.0),
                            axis=0, keepdims=True)
        rank_ref[k:k + 1, :] = r.astype(jnp.int32)
    carry_ref[...] = carry_ref[...] + jnp.sum(sel, axis=1, keepdims=True)
    cnt_ref[...] = carry_ref[...]


def _dispatch_plan(eidx_t, rank_t, counts, n_blocks):
    mb = MOE_MB
    cnt = counts[:, 0].astype(jnp.int32)
    padded = (cnt + mb - 1) // mb * mb
    pend = jnp.cumsum(padded)
    pstart = pend - padded
    experts = jnp.arange(N_EXPERTS, dtype=jnp.int32)
    slot_t = rank_t + jnp.sum(jnp.where(eidx_t[:, :, None] == experts, pstart, 0), axis=-1)
    block_end = pend // mb
    blocks = jnp.arange(n_blocks, dtype=jnp.int32)
    block_expert = jnp.minimum(jnp.sum((block_end[None, :] <= blocks[:, None]).astype(jnp.int32), axis=1),
                               N_EXPERTS - 1).astype(jnp.int32)
    n_used = block_end[-1:].astype(jnp.int32)
    last_block = (block_end - 1).astype(jnp.int32)
    has_rows = (padded > 0).astype(jnp.int32)
    return slot_t.astype(jnp.int32), block_expert, n_used, last_block, has_rows


def _dispatch_kernel(last_ref, has_ref, nused_ref, slot_ref, x_ref, xs_ref, zero_ref, zsem, sem, *, n_blocks):
    mb = MOE_MB
    tm = x_ref.shape[0]

    def zero_copy(block):
        return pltpu.make_async_copy(zero_ref, xs_ref.at[pl.ds(pl.multiple_of(block * mb, mb), mb)], zsem)

    @pl.when(pl.program_id(0) == 0)
    def _():
        zero_ref[...] = jnp.zeros_like(zero_ref)

        def start_e(e, c):
            @pl.when(has_ref[e] > 0)
            def _():
                zero_copy(last_ref[e]).start()
            return c

        def wait_e(e, c):
            @pl.when(has_ref[e] > 0)
            def _():
                zero_copy(last_ref[e]).wait()
            return c

        def start_b(blk, c):
            zero_copy(blk).start()
            return c

        def wait_b(blk, c):
            zero_copy(blk).wait()
            return c

        lax.fori_loop(0, N_EXPERTS, start_e, 0)
        lax.fori_loop(nused_ref[0], n_blocks, start_b, 0)
        lax.fori_loop(0, N_EXPERTS, wait_e, 0)
        lax.fori_loop(nused_ref[0], n_blocks, wait_b, 0)

    def issue(grp, c):
        base = pl.multiple_of(grp * SUBLANES, SUBLANES)
        for j in range(SUBLANES):
            for k in range(TOP_K):
                slot = slot_ref[0, (base + j) * TOP_K + k]
                pltpu.make_async_copy(x_ref.at[pl.ds(base + j, 1)], xs_ref.at[pl.ds(slot, 1)], sem).start(priority=k % 2)
        return c

    lax.fori_loop(0, tm // SUBLANES, issue, 0)
    for k in range(TOP_K):
        pltpu.make_async_copy(x_ref, xs_ref.at[pl.ds(0, tm)], sem).wait()


def _slot_tiles(slot_t, tm):
    t = slot_t.shape[1]
    return slot_t.T.reshape(t // tm, 1, tm * TOP_K)


def _dispatch(xpk, slot_t, n_used, last_block, has_rows, n_blocks):
    t, hw = xpk.shape
    tm = _row_tile(t, 512)
    grid_spec = pltpu.PrefetchScalarGridSpec(
        num_scalar_prefetch=3,
        grid=(t // tm,),
        in_specs=[
            pl.BlockSpec((None, 1, tm * TOP_K), lambda i, *_: (i, 0, 0), memory_space=pltpu.SMEM),
            pl.BlockSpec((tm, hw), lambda i, *_: (i, 0)),
        ],
        out_specs=pl.BlockSpec(memory_space=pl.ANY),
        scratch_shapes=[pltpu.VMEM((MOE_MB, hw), jnp.uint32), pltpu.SemaphoreType.DMA, pltpu.SemaphoreType.DMA],
    )
    return pl.pallas_call(
        functools.partial(_dispatch_kernel, n_blocks=n_blocks),
        grid_spec=grid_spec,
        out_shape=jax.ShapeDtypeStruct((n_blocks * MOE_MB, hw), jnp.uint32),
        compiler_params=_cparams(("arbitrary",)),
        name="dispatch",
    )(last_block, has_rows, n_used, _slot_tiles(slot_t, tm), xpk)


def _swiglu_packed(pk, wg, wu, wd):
    a, b = _unpack_bf16_pair(pk)
    gate = jnp.dot(a, wg[:HALF_D], preferred_element_type=F32) + jnp.dot(b, wg[HALF_D:], preferred_element_type=F32)
    up = jnp.dot(a, wu[:HALF_D], preferred_element_type=F32) + jnp.dot(b, wu[HALF_D:], preferred_element_type=F32)
    hid = (gate * jax.nn.sigmoid(gate) * up).astype(BF16)
    return jnp.dot(hid, wd, preferred_element_type=F32)


def _experts_kernel(be_ref, nused_ref, nxt_ref, xs_ref, wg_hbm, wu_hbm, wd_hbm, y_ref,
                    sg_ref, su_ref, sd_ref, wgb_ref, wub_ref, wdb_ref, sems, grp_ref, *, layer):
    i = pl.program_id(0)
    e = be_ref[i]
    first = (i == 0) | (e != be_ref[jnp.maximum(i - 1, 0)])

    def fetch(expert, slot):
        pairs = ((wg_hbm, sg_ref), (wu_hbm, su_ref), (wd_hbm, sd_ref))
        return [pltpu.make_async_copy(w.at[layer, expert], s.at[slot], sems.at[slot, j])
                for j, (w, s) in enumerate(pairs)]

    @pl.when(i == 0)
    def _():
        grp_ref[0] = 0
        for c in fetch(e, 0):
            c.start()

    @pl.when(first)
    def _():
        slot = grp_ref[0] & 1
        for c in fetch(e, slot):
            c.wait()
        nxt = nxt_ref[i]

        @pl.when(nxt != e)
        def _():
            for c in fetch(nxt, 1 - slot):
                c.start()

        wgb_ref[...] = sg_ref[slot].astype(BF16)
        wub_ref[...] = su_ref[slot].astype(BF16)
        wdb_ref[...] = sd_ref[slot].astype(BF16)
        grp_ref[0] = grp_ref[0] + 1

    @pl.when(i < nused_ref[0])
    def _():
        y_ref[...] = _swiglu_packed(xs_ref[...], wgb_ref, wub_ref, wdb_ref[...])

    @pl.when(i >= nused_ref[0])
    def _():
        y_ref[...] = jnp.zeros_like(y_ref)


def _next_expert(block_expert):
    experts = jnp.arange(N_EXPERTS, dtype=jnp.int32)
    present = jnp.any(block_expert[:, None] == experts[None, :], axis=0)
    later = jnp.where((experts[None, :] > block_expert[:, None]) & present[None, :], experts[None, :], N_EXPERTS)
    nxt = jnp.min(later, axis=1)
    return jnp.where(nxt == N_EXPERTS, block_expert, nxt).astype(jnp.int32)


def _experts(xs, block_expert, n_used, wg, wu, wd, layer, n_blocks):
    hw = xs.shape[1]
    d = wd.shape[3]
    hbm = pl.BlockSpec(memory_space=pl.ANY)
    grid_spec = pltpu.PrefetchScalarGridSpec(
        num_scalar_prefetch=3,
        grid=(n_blocks,),
        in_specs=[pl.BlockSpec((MOE_MB, hw), lambda i, *_: (i, 0)), hbm, hbm, hbm],
        out_specs=pl.BlockSpec((MOE_MB, d), lambda i, *_: (i, 0)),
        scratch_shapes=[pltpu.VMEM((2,) + wg.shape[2:], F32), pltpu.VMEM((2,) + wu.shape[2:], F32),
                        pltpu.VMEM((2,) + wd.shape[2:], F32),
                        pltpu.VMEM(wg.shape[2:], BF16), pltpu.VMEM(wu.shape[2:], BF16), pltpu.VMEM(wd.shape[2:], BF16),
                        pltpu.SemaphoreType.DMA((2, 3)), pltpu.SMEM((1,), jnp.int32)],
    )
    return pl.pallas_call(
        functools.partial(_experts_kernel, layer=layer),
        grid_spec=grid_spec,
        out_shape=jax.ShapeDtypeStruct((n_blocks * MOE_MB, d), F32),
        compiler_params=_cparams(("arbitrary",), VMEM_LIMIT),
        name="experts",
    )(block_expert, n_used, _next_expert(block_expert), xs, wg, wu, wd)


def _combine_kernel(slot_ref, gates_ref, x_ref, xpk_ref, ys_ref, wg_ref, wu_ref, wd_ref, g_ref, b_ref,
                    *rest, n_prompt_blocks, final):
    outs, bufs, sem = rest[:2], rest[2:2 + TOP_K], rest[2 + TOP_K]
    tm, d = x_ref.shape
    ngroups = tm // SUBLANES

    def issue(grp, c):
        for j in range(SUBLANES):
            for k in range(TOP_K):
                slot = slot_ref[0, (grp * SUBLANES + j) * TOP_K + k]
                pltpu.make_async_copy(ys_ref.at[pl.ds(slot, 1)], bufs[k].at[grp, pl.ds(j, 1)], sem).start(priority=k % 2)
        return c

    lax.fori_loop(0, ngroups, issue, 0)
    shared = _swiglu_packed(xpk_ref[...], wg_ref, wu_ref, wd_ref[...])

    def drain(grp, c):
        for k in range(TOP_K):
            pltpu.make_async_copy(ys_ref.at[pl.ds(0, SUBLANES)], bufs[k].at[grp], sem).wait()
        return c

    lax.fori_loop(0, ngroups, drain, 0)
    gates = gates_ref[...]
    routed = gates[:, 0:1] * bufs[0][...].reshape(tm, d)
    for k in range(1, TOP_K):
        routed = routed + gates[:, k:k + 1] * bufs[k][...].reshape(tm, d)
    y = _ln_rows(ALPHA * x_ref[...] + (routed + shared), g_ref[...], b_ref[...])
    if final:
        i = pl.program_id(0)

        @pl.when(i < n_prompt_blocks)
        def _():
            outs[0][...] = y

        @pl.when(i >= n_prompt_blocks)
        def _():
            outs[1][...] = y
    else:
        outs[0][...] = y
        outs[1][...] = y.astype(BF16)


def _combine(slot_t, gates, x1, xpk, ys, wg, wu, wd, g, b, t_prompt, final):
    t, d = x1.shape
    tm = _row_tile(math.gcd(t_prompt, t - t_prompt), 256)
    npb = t_prompt // tm
    nsb = (t - t_prompt) // tm
    vec = pl.BlockSpec((1, d), lambda i: (0, 0))
    full = lambda a: pl.BlockSpec(a.shape, lambda i: (0,) * a.ndim)
    if final:
        out_specs = [pl.BlockSpec((tm, d), lambda i: (jnp.minimum(i, npb - 1), 0)),
                     pl.BlockSpec((tm, d), lambda i: (jnp.maximum(i - npb, 0), 0))]
        out_shape = [jax.ShapeDtypeStruct((t_prompt, d), F32), jax.ShapeDtypeStruct((t - t_prompt, d), F32)]
    else:
        out_specs = [pl.BlockSpec((tm, d), lambda i: (i, 0)), pl.BlockSpec((tm, d), lambda i: (i, 0))]
        out_shape = [jax.ShapeDtypeStruct((t, d), F32), jax.ShapeDtypeStruct((t, d), BF16)]
    return pl.pallas_call(
        functools.partial(_combine_kernel, n_prompt_blocks=npb, final=final),
        grid=(npb + nsb,),
        in_specs=[
            pl.BlockSpec((None, 1, tm * TOP_K), lambda i: (i, 0, 0), memory_space=pltpu.SMEM),
            pl.BlockSpec((tm, TOP_K), lambda i: (i, 0)),
            pl.BlockSpec((tm, d), lambda i: (i, 0)),
            pl.BlockSpec((tm, d // 2), lambda i: (i, 0)),
            pl.BlockSpec(memory_space=pl.ANY),
            full(wg), full(wu), full(wd), vec, vec,
        ],
        out_specs=out_specs,
        out_shape=out_shape,
        scratch_shapes=[pltpu.VMEM((tm // SUBLANES, SUBLANES, d), F32)] * TOP_K + [pltpu.SemaphoreType.DMA],
        compiler_params=_cparams(("arbitrary",), VMEM_LIMIT),
        name="combine",
    )(_slot_tiles(slot_t, tm), gates, x1, xpk, ys, wg, wu, wd, g, b)


def _filter_features(seq):
    t = np.linspace(0.0, 1.0, seq)[:, None]
    w = 2.0 * np.pi * np.arange(seq) / seq
    bands = np.linspace(1e-4, FILTER_BANDS - 1, FILTER_BANDS)
    ang = w[:, None] * bands[None, :]
    z = np.zeros((seq, FILTER_ORDER))
    z[:, :FILTER_EMB] = np.concatenate([t, np.cos(ang), -np.sin(ang)], axis=-1)
    return jnp.asarray(z, F32)


def _decay_rates():
    max_decay = math.log(DECAY_TARGET) / SHORT_DECAY_PCT
    min_decay = math.log(DECAY_TARGET) / LONG_DECAY_PCT
    return jnp.asarray(np.linspace(min_decay, max_decay, HYENA_WIDTH)[None, :], F32)


def kernel(x_prompt, x_sample, ln_in_g, ln_in_b, w_in, attn_sink, hy_conv_w, hy_conv_b, hy_f_w1, hy_f_b1,
           hy_f_freq, hy_f_w2, hy_f_b2, hy_f_w3, hy_skip, pool_w, pool_scale, out_norm_g, w_out, ln1_g, ln1_b,
           router_w, router_bias, exp_w_gate, exp_w_up, exp_w_down, sh_w_gate, sh_w_up, sh_w_down, ln2_g, ln2_b):
    bp, seq, d = x_prompt.shape
    bs = x_sample.shape[0]
    assert x_sample.shape[1] == seq and d == D_MODEL and seq % (FFT_N1 * SUBLANES) == 0
    nbatch = bp + bs
    t_prompt = bp * seq
    t = nbatch * seq
    n_blocks = t * TOP_K // MOE_MB + N_EXPERTS
    assert (t * TOP_K) % MOE_MB == 0

    plan = _FftPlan(seq)
    ctab, stab = _rope_tables(seq)
    w1p = jnp.pad(hy_f_w1, ((0, 0), (0, FILTER_ORDER - FILTER_EMB), (0, 0)))
    hspec = _hyena_filter_spectrum(plan, _filter_features(seq), w1p, hy_f_b1[:, None, :], hy_f_freq[:, None, :],
                                   hy_f_w2, hy_f_b2[:, None, :], hy_f_w3, _decay_rates())

    row = lambda v: v.reshape(1, -1)
    x, xb = _ln_in(x_prompt.reshape(t_prompt, d), x_sample.reshape(bs * seq, d), row(ln_in_g), row(ln_in_b))
    outs = None
    for l in range(DEPTH):
        proj3 = _in_proj(xb, w_in[l].astype(BF16)).reshape(nbatch, seq, IN_WIDTH)
        ya = _attention(proj3, attn_sink[l], ctab, stab)
        hsrc, hgate = _hyena_pre(proj3, hy_conv_w[l], row(hy_conv_b[l]))
        yh = _hyena(plan, hsrc, hgate, row(hy_skip[l]), hspec, l)
        yp = _pool(proj3, pool_w[l].astype(BF16), row(pool_scale[l]))
        x1, xpk, eidx_t, gate_t, rank_t, counts = _post_mixer(
            ya.reshape(t, ATTN_WIDTH), yh.reshape(t, HYENA_WIDTH), yp.reshape(t, POOL_WIDTH), x,
            row(out_norm_g[l]), w_out[l].astype(BF16), row(ln1_g[l]), row(ln1_b[l]), router_w[l], router_bias[l])
        slot_t, block_expert, n_used, last_block, has_rows = _dispatch_plan(eidx_t, rank_t, counts, n_blocks)
        xs = _dispatch(xpk, slot_t, n_used, last_block, has_rows, n_blocks)
        ys = _experts(xs, block_expert, n_used, exp_w_gate, exp_w_up, exp_w_down, l, n_blocks)
        res = _combine(slot_t, gate_t.T, x1, xpk, ys, sh_w_gate[l].astype(BF16), sh_w_up[l].astype(BF16),
                       sh_w_down[l].astype(BF16), row(ln2_g[l]), row(ln2_b[l]), t_prompt, final=(l == DEPTH - 1))
        if l == DEPTH - 1:
            outs = res
        else:
            x, xb = res
    return outs[0].reshape(bp, seq, d), outs[1].reshape(bs, seq, d)
```

```python
import functools
import math

import jax
import jax.numpy as jnp
import numpy as np
from jax import lax
from jax.experimental import pallas as pl
from jax.experimental.pallas import tpu as pltpu

F32 = jnp.float32
BF16 = jnp.bfloat16
HIGHEST = lax.Precision.HIGHEST

D_MODEL = 2048
DEPTH = 2
HEAD_DIM = 128
ATTN_WIDTH = D_MODEL // 2
N_Q_HEADS = ATTN_WIDTH // HEAD_DIM
N_KV_HEADS = 2
Q_PER_KV = N_Q_HEADS // N_KV_HEADS
KV_WIDTH = N_KV_HEADS * HEAD_DIM
HYENA_WIDTH = D_MODEL // 4
POOL_WIDTH = D_MODEL - ATTN_WIDTH - HYENA_WIDTH
IN_WIDTH = ATTN_WIDTH + 2 * KV_WIDTH + 3 * HYENA_WIDTH + POOL_WIDTH
OFF_K = ATTN_WIDTH
OFF_V = OFF_K + KV_WIDTH
OFF_HY = OFF_V + KV_WIDTH
OFF_POOL = OFF_HY + 3 * HYENA_WIDTH

WINDOW = 128
BLOCK = 128
ROPE_THETA = 500000.0
ROPE_DIM = HEAD_DIM // 4
ROPE_HALF = ROPE_DIM // 2
NEG_BIG = -1e30
REMOVED = -3.0e38

SHORT_CONV = 3
FILTER_EMB = 33
FILTER_BANDS = (FILTER_EMB - 1) // 2
FILTER_ORDER = 64
DECAY_TARGET = 1e-2
SHORT_DECAY_PCT = 0.3
LONG_DECAY_PCT = 1.5

POOL_WINDOWS = (2, 4, 8, 16)
POOL_GROUP = POOL_WIDTH // len(POOL_WINDOWS)
OUT_GROUP = 128

N_EXPERTS = 64
TOP_K = 8
N_EXPERT_GROUPS = 8
EXPERTS_PER_GROUP = N_EXPERTS // N_EXPERT_GROUPS
TOPK_GROUPS = 4
EXPERT_HIDDEN = 512
SHARED_HIDDEN = 512
ROUTED_SCALE = 2.5

ALPHA = (2 * DEPTH) ** 0.25
LN_EPS = 1e-5
RMS_EPS = 1e-6

SUBLANES = 8
LANES = 128
VMEM_LIMIT = 56 * 1024 * 1024

FFT_N1 = 64
HY_CB = 128
MOE_MB = 512
HALF_D = D_MODEL // 2


def _cparams(sem, vmem=None):
    return pltpu.CompilerParams(dimension_semantics=sem, vmem_limit_bytes=vmem)


def _row_tile(t, pref):
    while t % pref:
        pref //= 2
    return pref


def _ln_rows(x, g, b):
    mu = jnp.mean(x, axis=-1, keepdims=True)
    xc = x - mu
    var = jnp.mean(xc * xc, axis=-1, keepdims=True)
    return xc * lax.rsqrt(var + LN_EPS) * g + b


def _pack_bf16_pair(x):
    h = x.shape[1] // 2
    hi = lax.bitcast_convert_type(x[:, :h].astype(BF16).astype(F32), jnp.uint32)
    lo = lax.bitcast_convert_type(x[:, h:].astype(BF16).astype(F32), jnp.uint32)
    return hi | (lo >> 16)


def _unpack_bf16_pair(pk):
    a = lax.bitcast_convert_type(pk & jnp.uint32(0xFFFF0000), F32).astype(BF16)
    b = lax.bitcast_convert_type(pk << 16, F32).astype(BF16)
    return a, b


def _ln_in_kernel(xp_ref, xs_ref, g_ref, b_ref, o_ref, ob_ref, *, n_prompt_blocks):
    i = pl.program_id(0)

    def emit(x):
        y = _ln_rows(x, g_ref[...], b_ref[...])
        o_ref[...] = y
        ob_ref[...] = y.astype(BF16)

    @pl.when(i < n_prompt_blocks)
    def _():
        emit(xp_ref[...])

    @pl.when(i >= n_prompt_blocks)
    def _():
        emit(xs_ref[...])


def _ln_in(xp, xs, g, b):
    tp, d = xp.shape
    ts = xs.shape[0]
    tm = _row_tile(math.gcd(tp, ts), 512)
    npb, nsb = tp // tm, ts // tm
    t = tp + ts
    return pl.pallas_call(
        functools.partial(_ln_in_kernel, n_prompt_blocks=npb),
        grid=(npb + nsb,),
        in_specs=[
            pl.BlockSpec((tm, d), lambda i: (jnp.minimum(i, npb - 1), 0)),
            pl.BlockSpec((tm, d), lambda i: (jnp.maximum(i - npb, 0), 0)),
            pl.BlockSpec((1, d), lambda i: (0, 0)),
            pl.BlockSpec((1, d), lambda i: (0, 0)),
        ],
        out_specs=[pl.BlockSpec((tm, d), lambda i: (i, 0)), pl.BlockSpec((tm, d), lambda i: (i, 0))],
        out_shape=[jax.ShapeDtypeStruct((t, d), F32), jax.ShapeDtypeStruct((t, d), BF16)],
        compiler_params=_cparams(("arbitrary",)),
        name="ln_in",
    )(xp, xs, g, b)


def _matmul_kernel(x_ref, w_ref, o_ref):
    o_ref[...] = jnp.dot(x_ref[...], w_ref[...], preferred_element_type=F32).astype(o_ref.dtype)


def _in_proj(xb, w):
    t, k = xb.shape
    n = w.shape[1]
    tm = _row_tile(t, 1024)
    tn = 512
    return pl.pallas_call(
        _matmul_kernel,
        grid=(t // tm, n // tn),
        in_specs=[pl.BlockSpec((tm, k), lambda i, j: (i, 0)), pl.BlockSpec((k, tn), lambda i, j: (0, j))],
        out_specs=pl.BlockSpec((tm, tn), lambda i, j: (i, j)),
        out_shape=jax.ShapeDtypeStruct((t, n), BF16),
        compiler_params=_cparams(("arbitrary", "arbitrary")),
        name="in_proj",
    )(xb, w)


def _rope_tables(seq):
    pos = np.arange(-BLOCK, seq + BLOCK, dtype=np.float64)
    inv = ROPE_THETA ** (-np.arange(ROPE_HALF, dtype=np.float64) / ROPE_HALF)
    ang = pos[:, None] * inv[None, :]
    c = np.ones((pos.shape[0], HEAD_DIM))
    s = np.zeros((pos.shape[0], HEAD_DIM))
    c[:, :ROPE_HALF] = np.cos(ang)
    c[:, ROPE_HALF:ROPE_DIM] = np.cos(ang)
    s[:, :ROPE_HALF] = -np.sin(ang)
    s[:, ROPE_HALF:ROPE_DIM] = np.sin(ang)
    return jnp.asarray(c, F32), jnp.asarray(s, F32)


def _rope(x, c, s):
    lane = lax.broadcasted_iota(jnp.int32, x.shape, 1)
    partner = jnp.where(lane < ROPE_HALF, pltpu.roll(x, HEAD_DIM - ROPE_HALF, 1), pltpu.roll(x, ROPE_HALF, 1))
    return x * c + partner * s


def _attn_kernel(sink_ref, q_ref, *refs, nb, qb):
    nkb = qb + 2
    k_refs, v_refs = refs[:nkb], refs[nkb:2 * nkb]
    ct_ref, st_ref, o_ref = refs[2 * nkb:]
    i = pl.program_id(1)
    base = pl.multiple_of(i * (qb * BLOCK), BLOCK)
    cw = ct_ref[pl.ds(base, nkb * BLOCK), :]
    sw = st_ref[pl.ds(base, nkb * BLOCK), :]
    kwin = jnp.concatenate([r[0] for r in k_refs], axis=0).astype(F32)
    vwin = jnp.concatenate([r[0] for r in v_refs], axis=0)
    q = q_ref[0].astype(F32)

    rows = Q_PER_KV * BLOCK
    rr = lax.broadcasted_iota(jnp.int32, (rows, 3 * BLOCK), 0) & (BLOCK - 1)
    cc = lax.broadcasted_iota(jnp.int32, (rows, 3 * BLOCK), 1)
    rel = cc - rr
    in_window = (rel >= BLOCK - WINDOW) & (rel <= BLOCK + WINDOW)
    rgrp = lax.broadcasted_iota(jnp.int32, (rows, 1), 0) // BLOCK

    for h in range(N_KV_HEADS):
        kh_all = _rope(kwin[:, h * HEAD_DIM:(h + 1) * HEAD_DIM], cw, sw).astype(BF16)
        vh_all = vwin[:, h * HEAD_DIM:(h + 1) * HEAD_DIM].astype(BF16)
        sink = jnp.zeros((rows, 1), F32)
        for g in range(Q_PER_KV):
            sink = jnp.where(rgrp == g, sink_ref[h * Q_PER_KV + g], sink)
        for sb in range(qb):
            blk = i * qb + sb
            lo = jnp.where(blk == 0, BLOCK, 0)
            hi = jnp.where(blk == nb - 1, 2 * BLOCK, 3 * BLOCK)
            valid = in_window & (cc >= lo) & (cc < hi)
            cq, sq = cw[(sb + 1) * BLOCK:(sb + 2) * BLOCK], sw[(sb + 1) * BLOCK:(sb + 2) * BLOCK]
            kh, vh = kh_all[sb * BLOCK:(sb + 3) * BLOCK], vh_all[sb * BLOCK:(sb + 3) * BLOCK]
            qs = []
            for g in range(Q_PER_KV):
                hq = h * Q_PER_KV + g
                qs.append(_rope(q[sb * BLOCK:(sb + 1) * BLOCK, hq * HEAD_DIM:(hq + 1) * HEAD_DIM], cq, sq).astype(BF16))
            qg = jnp.concatenate(qs, axis=0)
            s = lax.dot_general(qg, kh, (((1,), (1,)), ((), ())), preferred_element_type=F32) * (HEAD_DIM ** -0.5)
            s = jnp.where(valid, s, NEG_BIG)
            m = jnp.maximum(jnp.max(s, axis=-1, keepdims=True), sink)
            p = jnp.exp(s - m)
            denom = jnp.sum(p, axis=-1, keepdims=True) + jnp.exp(sink - m)
            o = jnp.dot((p / denom).astype(BF16), vh, preferred_element_type=F32)
            for g in range(Q_PER_KV):
                hq = h * Q_PER_KV + g
                o_ref[0, sb * BLOCK:(sb + 1) * BLOCK, hq * HEAD_DIM:(hq + 1) * HEAD_DIM] = o[g * BLOCK:(g + 1) * BLOCK]


def _attention(proj3, sink, ctab, stab):
    b, seq, _ = proj3.shape
    nb = seq // BLOCK
    qb = 2 if nb % 2 == 0 else 1
    kcol, vcol = OFF_K // KV_WIDTH, OFF_V // KV_WIDTH

    def kv_spec(col, shift):
        return pl.BlockSpec((1, BLOCK, KV_WIDTH), lambda bi, i: (bi, jnp.clip(i * qb + shift, 0, nb - 1), col))

    shifts = range(-1, qb + 1)
    n_kv = 2 * len(shifts)
    return pl.pallas_call(
        functools.partial(_attn_kernel, nb=nb, qb=qb),
        grid=(b, nb // qb),
        in_specs=[pl.BlockSpec(memory_space=pltpu.SMEM),
                  pl.BlockSpec((1, qb * BLOCK, ATTN_WIDTH), lambda bi, i: (bi, i, 0))]
                 + [kv_spec(kcol, s) for s in shifts] + [kv_spec(vcol, s) for s in shifts]
                 + [pl.BlockSpec(ctab.shape, lambda bi, i: (0, 0)), pl.BlockSpec(stab.shape, lambda bi, i: (0, 0))],
        out_specs=pl.BlockSpec((1, qb * BLOCK, ATTN_WIDTH), lambda bi, i: (bi, i, 0)),
        out_shape=jax.ShapeDtypeStruct((b, seq, ATTN_WIDTH), F32),
        compiler_params=_cparams(("arbitrary", "arbitrary")),
        name="attention",
    )(sink, *([proj3] * (1 + n_kv)), ctab, stab)


class _FftPlan:
    def __init__(self, seq):
        self.seq = seq
        self.n = 2 * seq
        self.n1 = FFT_N1
        self.n2 = self.n // self.n1
        self.nh = seq // self.n1
        self.k2 = self.n2 // 2 + 1
        self.k2p = -(-self.k2 // SUBLANES) * SUBLANES
        self.srow = 2 * self.n1 + SUBLANES
        n1 = np.arange(self.n1)
        n2 = np.arange(self.nh)
        k2 = np.arange(self.k2)
        npos = self.n1 * n2[None, None, :] + n1[:, None, None]
        ang = 2.0 * np.pi * k2[None, :, None] * npos / self.n
        f1 = np.zeros((self.n1, 2 * self.k2p, self.nh))
        f1[:, :self.k2] = np.cos(ang)
        f1[:, self.k2p:self.k2p + self.k2] = -np.sin(ang)
        wk = np.full((self.k2,), 2.0)
        wk[0] = 1.0
        wk[-1] = 1.0
        gm = np.zeros((self.n1, self.nh, 2 * self.k2p))
        angt = np.transpose(ang, (0, 2, 1))
        gm[:, :, :self.k2] = np.cos(angt) * wk / self.n
        gm[:, :, self.k2p:self.k2p + self.k2] = -np.sin(angt) * wk / self.n
        a2 = 2.0 * np.pi * np.outer(n1, n1) / self.n1
        c2, s2 = np.cos(a2), np.sin(a2)
        self.f1 = _split3_lhs(f1)
        self.g = _split3_lhs(gm)
        self.m2 = _split3_lhs(np.block([[c2, s2], [-s2, c2]]))
        self.m2i = _split3_lhs(np.block([[c2, -s2], [s2, c2]]))


def _dotf(a, b):
    return jnp.dot(a, b, precision=HIGHEST, preferred_element_type=F32)


def _split3_lhs(m):
    m32 = np.asarray(m, np.float32)
    hi = m32.astype(BF16)
    lo = (m32 - hi.astype(np.float32)).astype(BF16)
    return jnp.asarray(np.concatenate([hi, hi, lo], axis=-1))


def _split3_rhs(x):
    hi = x.astype(BF16)
    lo = (x - hi.astype(F32)).astype(BF16)
    return jnp.concatenate([hi, lo, hi], axis=0)


def _dot3(lhs3, x):
    return jnp.dot(lhs3, _split3_rhs(x), preferred_element_type=F32)


def _lane_cat(parts):
    return parts[0] if len(parts) == 1 else jnp.concatenate(parts, axis=1)


def _fft_stage1(plan, load_rows, f1_ref, slab_refs):
    unroll = 4

    def body(grp, carry):
        for u in range(unroll):
            n1 = grp * unroll + u
            res = _dot3(f1_ref[n1], _lane_cat([ld(n1) for ld in load_rows]))
            for p, slab in enumerate(slab_refs):
                part = res[:, p * LANES:(p + 1) * LANES]
                slab[pl.ds(n1, plan.k2p, stride=plan.srow), :] = part[:plan.k2p]
                slab[pl.ds(plan.n1 + n1, plan.k2p, stride=plan.srow), :] = part[plan.k2p:]
        return carry

    lax.fori_loop(0, plan.n1 // unroll, body, 0)


def _slab_rows(plan, slab_refs, k2):
    off = pl.multiple_of(k2 * plan.srow, SUBLANES)
    return off, _lane_cat([s[pl.ds(off, 2 * plan.n1), :] for s in slab_refs])


def _hyena_filter_kernel(z_ref, w1_ref, b1_ref, fr_ref, w2_ref, b2_ref, w3f_ref, w3b_ref, dl_ref,
                         f1_ref, m2_ref, h_ref, srcf_ref, srcb_ref, slabf_ref, slabb_ref, hid_ref, *, plan):
    seq = plan.seq

    @pl.when(pl.program_id(1) == 0)
    def _():
        fr = fr_ref[0]
        h1 = jnp.sin(fr * (_dotf(z_ref[...], w1_ref[0]) + b1_ref[0]))
        hid_ref[...] = jnp.sin(fr * (_dotf(h1, w2_ref[0]) + b2_ref[0]))

    hid = hid_ref[...]
    row = lax.broadcasted_iota(jnp.int32, (seq, 1), 0)
    t = row.astype(F32) * (1.0 / (seq - 1))
    decay = jnp.exp(-t * jnp.abs(dl_ref[...]))
    srcf_ref[...] = _dotf(hid, w3f_ref[0]) * decay
    srcb_ref[...] = jnp.where(row == 0, 0.0, _dotf(hid, w3b_ref[0]) * decay)
    slabs = (slabf_ref, slabb_ref)
    _fft_stage1(plan, [lambda n1, r=r: r[pl.ds(n1, plan.nh, stride=plan.n1), :] for r in (srcf_ref, srcb_ref)],
                f1_ref, slabs)

    unroll = 5 if plan.k2 % 5 == 0 else 1

    def body(grp, carry):
        for u in range(unroll):
            k2 = grp * unroll + u
            _, a = _slab_rows(plan, slabs, k2)
            x = _dot3(m2_ref[...], a)
            h_ref[0, k2, :plan.n1, :] = x[:plan.n1, :LANES] + x[:plan.n1, LANES:]
            h_ref[0, k2, plan.n1:, :] = x[plan.n1:, :LANES] - x[plan.n1:, LANES:]
        return carry

    lax.fori_loop(0, plan.k2 // unroll, body, 0)


def _hyena_filter_spectrum(plan, z, w1p, b1, freq, w2, b2, w3, deltas):
    depth = w3.shape[0]
    cb = HY_CB
    ncb = HYENA_WIDTH // cb
    seq = plan.seq
    lay = lambda shp: pl.BlockSpec((1,) + shp, lambda l, j: (l,) + (0,) * len(shp))
    return pl.pallas_call(
        functools.partial(_hyena_filter_kernel, plan=plan),
        grid=(depth, ncb),
        in_specs=[
            pl.BlockSpec(z.shape, lambda l, j: (0, 0)),
            lay(w1p.shape[1:]), lay(b1.shape[1:]), lay(freq.shape[1:]), lay(w2.shape[1:]), lay(b2.shape[1:]),
            pl.BlockSpec((1, FILTER_ORDER, cb), lambda l, j: (l, 0, j)),
            pl.BlockSpec((1, FILTER_ORDER, cb), lambda l, j: (l, 0, ncb + j)),
            pl.BlockSpec((1, cb), lambda l, j: (0, j)),
            pl.BlockSpec(plan.f1.shape, lambda l, j: (0, 0, 0)),
            pl.BlockSpec(plan.m2.shape, lambda l, j: (0, 0)),
        ],
        out_specs=pl.BlockSpec((1, plan.k2, 2 * plan.n1, cb), lambda l, j: (l, 0, 0, j)),
        out_shape=jax.ShapeDtypeStruct((depth, plan.k2, 2 * plan.n1, HYENA_WIDTH), F32),
        scratch_shapes=[pltpu.VMEM((seq, cb), F32), pltpu.VMEM((seq, cb), F32),
                        pltpu.VMEM((plan.k2p * plan.srow, cb), F32), pltpu.VMEM((plan.k2p * plan.srow, cb), F32),
                        pltpu.VMEM((seq, FILTER_ORDER), F32)],
        compiler_params=_cparams(("arbitrary", "arbitrary"), VMEM_LIMIT),
        name="hyena_filter",
    )(z, w1p, b1, freq, w2, b2, w3, w3, deltas, plan.f1, plan.m2)


def _short_conv(u, w, b, seq):
    row = lax.broadcasted_iota(jnp.int32, (seq, 1), 0)
    prev = jnp.where(row >= 1, pltpu.roll(u, 1, 0), 0.0)
    nxt = jnp.where(row < seq - 1, pltpu.roll(u, seq - 1, 0), 0.0)
    return prev * w[0:1] + u * w[1:2] + nxt * w[2:3] + b


def _hyena_pre_kernel(x0_ref, x1_ref, v_ref, w0_ref, w1_ref, wv_ref, b0_ref, b1_ref, bv_ref, src_ref, gate_ref, *, seq):
    gate_ref[0] = _short_conv(x0_ref[0].astype(F32), w0_ref[...], b0_ref[...], seq)
    src_ref[0] = (_short_conv(v_ref[0].astype(F32), wv_ref[...], bv_ref[...], seq)
                  * _short_conv(x1_ref[0].astype(F32), w1_ref[...], b1_ref[...], seq))


def _hyena_pre(proj3, conv_w, conv_b):
    b, seq, _ = proj3.shape
    cb = HY_CB
    ncb = HYENA_WIDTH // cb
    c0 = OFF_HY // cb

    def u_spec(part):
        return pl.BlockSpec((1, seq, cb), lambda bi, j: (bi, 0, c0 + part * ncb + j))

    def w_spec(part):
        return pl.BlockSpec((SHORT_CONV, cb), lambda bi, j: (0, part * ncb + j))

    def b_spec(part):
        return pl.BlockSpec((1, cb), lambda bi, j: (0, part * ncb + j))

    out = pl.BlockSpec((1, seq, cb), lambda bi, j: (bi, 0, j))
    shp = jax.ShapeDtypeStruct((b, seq, HYENA_WIDTH), F32)
    return pl.pallas_call(
        functools.partial(_hyena_pre_kernel, seq=seq),
        grid=(b, ncb),
        in_specs=[u_spec(0), u_spec(1), u_spec(2), w_spec(0), w_spec(1), w_spec(2), b_spec(0), b_spec(1), b_spec(2)],
        out_specs=[out, out],
        out_shape=[shp, shp],
        compiler_params=_cparams(("arbitrary", "arbitrary"), VMEM_LIMIT),
        name="hyena_pre",
    )(proj3, proj3, proj3, conv_w, conv_w, conv_w, conv_b, conv_b, conv_b)


def _hyena_kernel(src_ref, gate_ref, skip_ref, h_ref, f1_ref, m2_ref, m2i_ref, g_ref, o_ref, *slab_refs, plan):
    npar = len(slab_refs)
    n1c = plan.n1
    _fft_stage1(plan, [lambda n1, p=p: src_ref[p, pl.ds(n1, plan.nh, stride=n1c), :] for p in range(npar)],
                f1_ref, slab_refs)

    unroll2 = 5 if plan.k2 % 5 == 0 else 1

    def stage2(grp, carry):
        for u in range(unroll2):
            k2 = grp * unroll2 + u
            off, a = _slab_rows(plan, slab_refs, k2)
            x = _dot3(m2_ref[...], a)
            h = _lane_cat([h_ref[0, k2]] * npar)
            xr, xi = x[:n1c], x[n1c:]
            hr, hi = h[:n1c], h[n1c:]
            y = jnp.concatenate([xr * hr - xi * hi, xr * hi + xi * hr], axis=0)
            back = _dot3(m2i_ref[...], y)
            for p, slab in enumerate(slab_refs):
                slab[pl.ds(off, 2 * n1c), :] = back[:, p * LANES:(p + 1) * LANES]
        return carry

    lax.fori_loop(0, plan.k2 // unroll2, stage2, 0)

    unroll3 = 4

    def stage3(grp, carry):
        for u in range(unroll3):
            n1 = grp * unroll3 + u
            a = _lane_cat([jnp.concatenate([s[pl.ds(n1, plan.k2p, stride=plan.srow), :],
                                            s[pl.ds(n1c + n1, plan.k2p, stride=plan.srow), :]], axis=0)
                           for s in slab_refs])
            y = _dot3(g_ref[n1], a)
            for p in range(npar):
                o_ref[p, pl.ds(n1, plan.nh, stride=n1c), :] = y[:, p * LANES:(p + 1) * LANES]
        return carry

    lax.fori_loop(0, n1c // unroll3, stage3, 0)
    for p in range(npar):
        o_ref[p] = (o_ref[p] + src_ref[p] * skip_ref[...]) * gate_ref[p]


def _hyena(plan, src, gate, skip, hspec, layer):
    b, seq, _ = src.shape
    cb = HY_CB
    ncb = HYENA_WIDTH // cb
    npar = 2 if b % 2 == 0 else 1
    once = pl.Buffered(1)
    const = lambda a: pl.BlockSpec(a.shape, lambda j, bi: (0,) * a.ndim, pipeline_mode=once)
    seq_spec = pl.BlockSpec((npar, seq, cb), lambda j, bi: (bi, 0, j))
    return pl.pallas_call(
        functools.partial(_hyena_kernel, plan=plan),
        grid=(ncb, b // npar),
        in_specs=[
            seq_spec, seq_spec,
            pl.BlockSpec((1, cb), lambda j, bi: (0, j)),
            pl.BlockSpec((1, plan.k2, 2 * plan.n1, cb), lambda j, bi: (layer, 0, 0, j), pipeline_mode=once),
            const(plan.f1), const(plan.m2), const(plan.m2i), const(plan.g),
        ],
        out_specs=seq_spec,
        out_shape=jax.ShapeDtypeStruct((b, seq, HYENA_WIDTH), F32),
        scratch_shapes=[pltpu.VMEM((plan.k2p * plan.srow, cb), F32)] * npar,
        compiler_params=_cparams(("arbitrary", "arbitrary"), VMEM_LIMIT),
        name="hyena",
    )(src, gate, skip, hspec, plan.f1, plan.m2, plan.m2i, plan.g)


def _pool_kernel(u_ref, w_ref, sc_ref, o_ref, *, seq):
    row = lax.broadcasted_iota(jnp.int32, (seq, 1), 0)

    def back(x, s):
        return jnp.where(row >= s, pltpu.roll(x, s, 0), 0.0)

    def fwd(x, s):
        return jnp.where(row < seq - s, pltpu.roll(x, seq - s, 0), 0.0)

    for gi, win in enumerate(POOL_WINDOWS):
        half = win // 2
        u = u_ref[0, :, gi * POOL_GROUP:(gi + 1) * POOL_GROUP].astype(F32)
        ahead, behind, s = u, u, 1
        while s < half:
            ahead = ahead + fwd(ahead, s)
            behind = behind + back(behind, s)
            s *= 2
        total = ahead + back(behind, 1)
        cnt = (jnp.minimum(row + half, seq) - jnp.maximum(row - half, 0)).astype(F32)
        diff = total / cnt - u
        y = jnp.dot(diff.astype(BF16), w_ref[gi], preferred_element_type=F32)
        o_ref[0, :, gi * POOL_GROUP:(gi + 1) * POOL_GROUP] = y * sc_ref[:, gi * POOL_GROUP:(gi + 1) * POOL_GROUP]


def _pool(proj3, pool_w, pool_scale):
    b, seq, _ = proj3.shape
    return pl.pallas_call(
        functools.partial(_pool_kernel, seq=seq),
        grid=(b,),
        in_specs=[
            pl.BlockSpec((1, seq, POOL_WIDTH), lambda bi: (bi, 0, OFF_POOL // POOL_WIDTH)),
            pl.BlockSpec(pool_w.shape, lambda bi: (0, 0, 0)),
            pl.BlockSpec((1, POOL_WIDTH), lambda bi: (0, 0)),
        ],
        out_specs=pl.BlockSpec((1, seq, POOL_WIDTH), lambda bi: (bi, 0, 0)),
        out_shape=jax.ShapeDtypeStruct((b, seq, POOL_WIDTH), F32),
        compiler_params=_cparams(("arbitrary",), VMEM_LIMIT),
        name="pool",
    )(proj3, pool_w, pool_scale)


def _post_mixer_kernel(ya_ref, yh_ref, yp_ref, x_ref, gn_ref, w_ref, g_ref, b_ref, rw_ref, bias_ref, tri_ref,
                       o_ref, opk_ref, eidx_ref, gate_ref, rank_ref, cnt_ref, carry_ref):
    parts = []
    for src in (ya_ref, yh_ref, yp_ref):
        for j in range(src.shape[1] // OUT_GROUP):
            c = src[:, j * OUT_GROUP:(j + 1) * OUT_GROUP]
            parts.append(c * lax.rsqrt(jnp.mean(c * c, axis=-1, keepdims=True) + RMS_EPS))
    yn = (jnp.concatenate(parts, axis=-1) * gn_ref[...]).astype(BF16)
    mix = jnp.dot(yn, w_ref[...], preferred_element_type=F32)
    x1 = _ln_rows(ALPHA * x_ref[...] + mix, g_ref[...], b_ref[...])
    o_ref[...] = x1
    opk_ref[...] = _pack_bf16_pair(x1)
    _route(x1, rw_ref, bias_ref, tri_ref, eidx_ref, gate_ref, rank_ref, cnt_ref, carry_ref)


def _post_mixer(ya, yh, yp, x, gn, w_out, g, b, router_w, router_bias):
    t, d = x.shape
    tm = _row_tile(t, 512)
    row = lambda w: pl.BlockSpec((tm, w), lambda i: (i, 0))
    vec = pl.BlockSpec((1, d), lambda i: (0, 0))
    once = pl.Buffered(1)
    const = lambda a: pl.BlockSpec(a.shape, lambda i: (0,) * a.ndim, pipeline_mode=once)
    rw3 = _split3_lhs_traced(router_w.T)
    bias_b = jnp.broadcast_to(router_bias[:, None], (N_EXPERTS, tm)).astype(F32)
    tri = jnp.asarray(np.triu(np.ones((tm, tm)), 1), BF16)
    kt = lambda dt: jax.ShapeDtypeStruct((TOP_K, t), dt)
    kspec = pl.BlockSpec((TOP_K, tm), lambda i: (0, i))
    return pl.pallas_call(
        _post_mixer_kernel,
        grid=(t // tm,),
        in_specs=[row(ATTN_WIDTH), row(HYENA_WIDTH), row(POOL_WIDTH), row(d), vec, const(w_out), vec, vec,
                  const(rw3), const(bias_b), const(tri)],
        out_specs=[row(d), row(d // 2), kspec, kspec, kspec, pl.BlockSpec((N_EXPERTS, LANES), lambda i: (0, 0))],
        out_shape=[jax.ShapeDtypeStruct((t, d), F32), jax.ShapeDtypeStruct((t, d // 2), jnp.uint32),
                   kt(jnp.int32), kt(F32), kt(jnp.int32), jax.ShapeDtypeStruct((N_EXPERTS, LANES), F32)],
        scratch_shapes=[pltpu.VMEM((N_EXPERTS, LANES), F32)],
        compiler_params=_cparams(("arbitrary",), VMEM_LIMIT),
        name="post_mixer",
    )(ya, yh, yp, x, gn, w_out, g, b, rw3, bias_b, tri)


def _split3_lhs_traced(m):
    hi = m.astype(BF16)
    lo = (m - hi.astype(F32)).astype(BF16)
    return jnp.concatenate([hi, hi, lo], axis=-1)


def _route(x1, rw_ref, bias_ref, tri_ref, eidx_ref, gate_ref, rank_ref, cnt_ref, carry_ref):
    tm = x1.shape[0]

    @pl.when(pl.program_id(0) == 0)
    def _():
        carry_ref[...] = jnp.zeros_like(carry_ref)

    xh = x1.astype(BF16)
    xl = (x1 - xh.astype(F32)).astype(BF16)
    logits = lax.dot_general(rw_ref[...], jnp.concatenate([xh, xl, xh], axis=1), (((1,), (1,)), ((), ())),
                             preferred_element_type=F32)
    scores = jax.nn.sigmoid(logits)
    choice = scores + bias_ref[...]
    sub = lax.broadcasted_iota(jnp.int32, (EXPERTS_PER_GROUP, tm), 0)
    far = jnp.int32(N_EXPERTS)

    def first_argmax(v):
        m = jnp.max(v, axis=0, keepdims=True)
        return m, jnp.min(jnp.where(v == m, sub, far), axis=0, keepdims=True)

    tiles, stiles, gscore = [], [], []
    for g in range(N_EXPERT_GROUPS):
        c = choice[g * EXPERTS_PER_GROUP:(g + 1) * EXPERTS_PER_GROUP]
        tiles.append(c)
        stiles.append(scores[g * EXPERTS_PER_GROUP:(g + 1) * EXPERTS_PER_GROUP])
        m1, i1 = first_argmax(c)
        m2 = jnp.max(jnp.where(sub == i1, REMOVED, c), axis=0, keepdims=True)
        gscore.append(m1 + m2)
    cur = jnp.concatenate(gscore, axis=0)
    gsel = jnp.zeros(cur.shape, F32)
    for _ in range(TOPK_GROUPS):
        _, ig = first_argmax(cur)
        hit = sub == ig
        gsel = jnp.where(hit, 1.0, gsel)
        cur = jnp.where(hit, REMOVED, cur)

    eids = [sub + g * EXPERTS_PER_GROUP for g in range(N_EXPERT_GROUPS)]
    masked = [jnp.where(gsel[g:g + 1] > 0.5, tiles[g], NEG_BIG) for g in range(N_EXPERT_GROUPS)]
    picked = [jnp.zeros((EXPERTS_PER_GROUP, tm), F32) for _ in range(N_EXPERT_GROUPS)]
    idxs, sels = [], []
    for _ in range(TOP_K):
        m = functools.reduce(jnp.maximum, [jnp.max(v, axis=0, keepdims=True) for v in masked])
        idx = functools.reduce(jnp.minimum, [jnp.min(jnp.where(v == m, e, far), axis=0, keepdims=True)
                                             for v, e in zip(masked, eids)])
        sc = jnp.zeros((1, tm), F32)
        for g in range(N_EXPERT_GROUPS):
            hit = eids[g] == idx
            sc = sc + jnp.sum(jnp.where(hit, stiles[g], 0.0), axis=0, keepdims=True)
            masked[g] = jnp.where(hit, REMOVED, masked[g])
            picked[g] = jnp.where(hit, 1.0, picked[g])
        idxs.append(idx)
        sels.append(sc)
    total = functools.reduce(lambda a, c: a + c, sels)
    for k in range(TOP_K):
        eidx_ref[k:k + 1, :] = idxs[k]
        gate_ref[k:k + 1, :] = sels[k] / total * ROUTED_SCALE

    sel = jnp.concatenate(picked, axis=0)
    before = jnp.dot(sel.astype(BF16), tri_ref[...], preferred_element_type=F32) + carry_ref[:, 0:1]
    for k in range(TOP_K):
        r = jnp.zeros((1, tm), F32)
        for g in range(N_EXPERT_GROUPS):
            r = r + jnp.sum(jnp.where(eids[g] == idxs[k], before[g * EXPERTS_PER_GROUP:(g + 1) * EXPERTS_PER_GROUP], 0.0),
                            axis=0, keepdims=True)
        rank_ref[k:k + 1, :] = r.astype(jnp.int32)
    carry_ref[...] = carry_ref[...] + jnp.sum(sel, axis=1, keepdims=True)
    cnt_ref[...] = carry_ref[...]


def _dispatch_plan(eidx_t, rank_t, counts, n_blocks):
    mb = MOE_MB
    cnt = counts[:, 0].astype(jnp.int32)
    padded = (cnt + mb - 1) // mb * mb
    pend = jnp.cumsum(padded)
    pstart = pend - padded
    experts = jnp.arange(N_EXPERTS, dtype=jnp.int32)
    slot_t = rank_t + jnp.sum(jnp.where(eidx_t[:, :, None] == experts, pstart, 0), axis=-1)
    block_end = pend // mb
    blocks = jnp.arange(n_blocks, dtype=jnp.int32)
    block_expert = jnp.minimum(jnp.sum((block_end[None, :] <= blocks[:, None]).astype(jnp.int32), axis=1),
                               N_EXPERTS - 1).astype(jnp.int32)
    n_used = block_end[-1:].astype(jnp.int32)
    last_block = (block_end - 1).astype(jnp.int32)
    has_rows = (padded > 0).astype(jnp.int32)
    return slot_t.astype(jnp.int32), block_expert, n_used, last_block, has_rows


def _dispatch_kernel(last_ref, has_ref, nused_ref, slot_ref, x_ref, wg_ref, wu_ref, wd_ref, xs_ref, sh_ref,
                     zero_ref, zsem, sem, *, n_blocks):
    mb = MOE_MB
    tm = x_ref.shape[0]

    def zero_copy(block):
        return pltpu.make_async_copy(zero_ref, xs_ref.at[pl.ds(pl.multiple_of(block * mb, mb), mb)], zsem)

    @pl.when(pl.program_id(0) == 0)
    def _():
        zero_ref[...] = jnp.zeros_like(zero_ref)

        def start_e(e, c):
            @pl.when(has_ref[e] > 0)
            def _():
                zero_copy(last_ref[e]).start()
            return c

        def wait_e(e, c):
            @pl.when(has_ref[e] > 0)
            def _():
                zero_copy(last_ref[e]).wait()
            return c

        def start_b(blk, c):
            zero_copy(blk).start()
            return c

        def wait_b(blk, c):
            zero_copy(blk).wait()
            return c

        lax.fori_loop(0, N_EXPERTS, start_e, 0)
        lax.fori_loop(nused_ref[0], n_blocks, start_b, 0)
        lax.fori_loop(0, N_EXPERTS, wait_e, 0)
        lax.fori_loop(nused_ref[0], n_blocks, wait_b, 0)

    groups = list(range(tm // SUBLANES))

    def scatter_rows(n_groups):
        for grp in groups[:n_groups]:
            for j in range(SUBLANES):
                r = grp * SUBLANES + j
                for k in range(TOP_K):
                    slot = slot_ref[0, r * TOP_K + k]
                    pltpu.make_async_copy(x_ref.at[pl.ds(r, 1)], xs_ref.at[pl.ds(slot, 1)], sem).start(priority=k % 2)
        del groups[:n_groups]

    chunk = 2 * LANES
    n_pieces = 2 * (SHARED_HIDDEN // chunk) + D_MODEL // chunk
    per_piece = -(-len(groups) // n_pieces)
    a, b = _unpack_bf16_pair(x_ref[...])
    pre = []
    for w_ref in (wg_ref, wu_ref):
        cols = []
        for c in range(SHARED_HIDDEN // chunk):
            scatter_rows(per_piece)
            cs = slice(c * chunk, (c + 1) * chunk)
            cols.append(jnp.dot(a, w_ref[:HALF_D, cs], preferred_element_type=F32)
                        + jnp.dot(b, w_ref[HALF_D:, cs], preferred_element_type=F32))
        pre.append(jnp.concatenate(cols, axis=1))
    hid = (pre[0] * jax.nn.sigmoid(pre[0]) * pre[1]).astype(BF16)
    for c in range(D_MODEL // chunk):
        scatter_rows(per_piece)
        cs = slice(c * chunk, (c + 1) * chunk)
        sh_ref[:, cs] = jnp.dot(hid, wd_ref[:, cs], preferred_element_type=F32)
    scatter_rows(len(groups))
    for k in range(TOP_K):
        pltpu.make_async_copy(x_ref, xs_ref.at[pl.ds(0, tm)], sem).wait()


def _slot_tiles(slot_t, tm):
    t = slot_t.shape[1]
    return slot_t.T.reshape(t // tm, 1, tm * TOP_K)


def _dispatch(xpk, slot_t, n_used, last_block, has_rows, wg, wu, wd, n_blocks):
    t, hw = xpk.shape
    tm = _row_tile(t, 256)
    full = lambda a: pl.BlockSpec(a.shape, lambda i, *_: (0,) * a.ndim)
    grid_spec = pltpu.PrefetchScalarGridSpec(
        num_scalar_prefetch=3,
        grid=(t // tm,),
        in_specs=[
            pl.BlockSpec((None, 1, tm * TOP_K), lambda i, *_: (i, 0, 0), memory_space=pltpu.SMEM),
            pl.BlockSpec((tm, hw), lambda i, *_: (i, 0)),
            full(wg), full(wu), full(wd),
        ],
        out_specs=[pl.BlockSpec(memory_space=pl.ANY), pl.BlockSpec((tm, D_MODEL), lambda i, *_: (i, 0))],
        scratch_shapes=[pltpu.VMEM((MOE_MB, hw), jnp.uint32), pltpu.SemaphoreType.DMA, pltpu.SemaphoreType.DMA],
    )
    return pl.pallas_call(
        functools.partial(_dispatch_kernel, n_blocks=n_blocks),
        grid_spec=grid_spec,
        out_shape=[jax.ShapeDtypeStruct((n_blocks * MOE_MB, hw), jnp.uint32), jax.ShapeDtypeStruct((t, D_MODEL), F32)],
        compiler_params=_cparams(("arbitrary",), VMEM_LIMIT),
        name="dispatch",
    )(last_block, has_rows, n_used, _slot_tiles(slot_t, tm), xpk, wg, wu, wd)


def _swiglu_packed(pk, wg, wu, wd):
    a, b = _unpack_bf16_pair(pk)
    gate = jnp.dot(a, wg[:HALF_D], preferred_element_type=F32) + jnp.dot(b, wg[HALF_D:], preferred_element_type=F32)
    up = jnp.dot(a, wu[:HALF_D], preferred_element_type=F32) + jnp.dot(b, wu[HALF_D:], preferred_element_type=F32)
    hid = (gate * jax.nn.sigmoid(gate) * up).astype(BF16)
    return jnp.dot(hid, wd, preferred_element_type=F32)


def _experts_kernel(be_ref, nused_ref, nxt_ref, xs_ref, wg_hbm, wu_hbm, wd_hbm, y_ref,
                    sg_ref, su_ref, sd_ref, wgb_ref, wub_ref, wdb_ref, sems, grp_ref, *, layer):
    i = pl.program_id(0)
    e = be_ref[i]
    first = (i == 0) | (e != be_ref[jnp.maximum(i - 1, 0)])

    def fetch(expert, slot):
        pairs = ((wg_hbm, sg_ref), (wu_hbm, su_ref), (wd_hbm, sd_ref))
        return [pltpu.make_async_copy(w.at[layer, expert], s.at[slot], sems.at[slot, j])
                for j, (w, s) in enumerate(pairs)]

    @pl.when(i == 0)
    def _():
        grp_ref[0] = 0
        for c in fetch(e, 0):
            c.start()

    @pl.when(first)
    def _():
        slot = grp_ref[0] & 1
        for c in fetch(e, slot):
            c.wait()
        nxt = nxt_ref[i]

        @pl.when(nxt != e)
        def _():
            for c in fetch(nxt, 1 - slot):
                c.start()

        wgb_ref[...] = sg_ref[slot].astype(BF16)
        wub_ref[...] = su_ref[slot].astype(BF16)
        wdb_ref[...] = sd_ref[slot].astype(BF16)
        grp_ref[0] = grp_ref[0] + 1

    @pl.when(i < nused_ref[0])
    def _():
        y_ref[...] = _swiglu_packed(xs_ref[...], wgb_ref, wub_ref, wdb_ref[...])

    @pl.when(i >= nused_ref[0])
    def _():
        y_ref[...] = jnp.zeros_like(y_ref)


def _next_expert(block_expert):
    experts = jnp.arange(N_EXPERTS, dtype=jnp.int32)
    present = jnp.any(block_expert[:, None] == experts[None, :], axis=0)
    later = jnp.where((experts[None, :] > block_expert[:, None]) & present[None, :], experts[None, :], N_EXPERTS)
    nxt = jnp.min(later, axis=1)
    return jnp.where(nxt == N_EXPERTS, block_expert, nxt).astype(jnp.int32)


def _experts(xs, block_expert, n_used, wg, wu, wd, layer, n_blocks):
    hw = xs.shape[1]
    d = wd.shape[3]
    hbm = pl.BlockSpec(memory_space=pl.ANY)
    grid_spec = pltpu.PrefetchScalarGridSpec(
        num_scalar_prefetch=3,
        grid=(n_blocks,),
        in_specs=[pl.BlockSpec((MOE_MB, hw), lambda i, *_: (i, 0)), hbm, hbm, hbm],
        out_specs=pl.BlockSpec((MOE_MB, d), lambda i, *_: (i, 0)),
        scratch_shapes=[pltpu.VMEM((2,) + wg.shape[2:], F32), pltpu.VMEM((2,) + wu.shape[2:], F32),
                        pltpu.VMEM((2,) + wd.shape[2:], F32),
                        pltpu.VMEM(wg.shape[2:], BF16), pltpu.VMEM(wu.shape[2:], BF16), pltpu.VMEM(wd.shape[2:], BF16),
                        pltpu.SemaphoreType.DMA((2, 3)), pltpu.SMEM((1,), jnp.int32)],
    )
    return pl.pallas_call(
        functools.partial(_experts_kernel, layer=layer),
        grid_spec=grid_spec,
        out_shape=jax.ShapeDtypeStruct((n_blocks * MOE_MB, d), F32),
        compiler_params=_cparams(("arbitrary",), VMEM_LIMIT),
        name="experts",
    )(block_expert, n_used, _next_expert(block_expert), xs, wg, wu, wd)


def _combine_kernel(slot_ref, gates_ref, x_ref, sh_ref, ys_ref, g_ref, b_ref, *rest, n_prompt_blocks, final):
    outs, bufs, sem = rest[:2], rest[2:2 + TOP_K], rest[2 + TOP_K]
    tm, d = x_ref.shape
    ngroups = tm // SUBLANES

    def issue(grp, c):
        for j in range(SUBLANES):
            for k in range(TOP_K):
                slot = slot_ref[0, (grp * SUBLANES + j) * TOP_K + k]
                pltpu.make_async_copy(ys_ref.at[pl.ds(slot, 1)], bufs[k].at[grp, pl.ds(j, 1)], sem).start(priority=k % 2)
        return c

    lax.fori_loop(0, ngroups, issue, 0)

    def drain(grp, c):
        for k in range(TOP_K):
            pltpu.make_async_copy(ys_ref.at[pl.ds(0, SUBLANES)], bufs[k].at[grp], sem).wait()
        return c

    lax.fori_loop(0, ngroups, drain, 0)
    gates = gates_ref[...]
    routed = gates[:, 0:1] * bufs[0][...].reshape(tm, d)
    for k in range(1, TOP_K):
        routed = routed + gates[:, k:k + 1] * bufs[k][...].reshape(tm, d)
    y = _ln_rows(ALPHA * x_ref[...] + (routed + sh_ref[...]), g_ref[...], b_ref[...])
    if final:
        i = pl.program_id(0)

        @pl.when(i < n_prompt_blocks)
        def _():
            outs[0][...] = y

        @pl.when(i >= n_prompt_blocks)
        def _():
            outs[1][...] = y
    else:
        outs[0][...] = y
        outs[1][...] = y.astype(BF16)


def _combine(slot_t, gates, x1, shared, ys, g, b, t_prompt, final):
    t, d = x1.shape
    tm = _row_tile(math.gcd(t_prompt, t - t_prompt), 256)
    npb = t_prompt // tm
    nsb = (t - t_prompt) // tm
    vec = pl.BlockSpec((1, d), lambda i: (0, 0))
    if final:
        out_specs = [pl.BlockSpec((tm, d), lambda i: (jnp.minimum(i, npb - 1), 0)),
                     pl.BlockSpec((tm, d), lambda i: (jnp.maximum(i - npb, 0), 0))]
        out_shape = [jax.ShapeDtypeStruct((t_prompt, d), F32), jax.ShapeDtypeStruct((t - t_prompt, d), F32)]
    else:
        out_specs = [pl.BlockSpec((tm, d), lambda i: (i, 0)), pl.BlockSpec((tm, d), lambda i: (i, 0))]
        out_shape = [jax.ShapeDtypeStruct((t, d), F32), jax.ShapeDtypeStruct((t, d), BF16)]
    return pl.pallas_call(
        functools.partial(_combine_kernel, n_prompt_blocks=npb, final=final),
        grid=(npb + nsb,),
        in_specs=[
            pl.BlockSpec((None, 1, tm * TOP_K), lambda i: (i, 0, 0), memory_space=pltpu.SMEM),
            pl.BlockSpec((tm, TOP_K), lambda i: (i, 0)),
            pl.BlockSpec((tm, d), lambda i: (i, 0)),
            pl.BlockSpec((tm, d), lambda i: (i, 0)),
            pl.BlockSpec(memory_space=pl.ANY),
            vec, vec,
        ],
        out_specs=out_specs,
        out_shape=out_shape,
        scratch_shapes=[pltpu.VMEM((tm // SUBLANES, SUBLANES, d), F32)] * TOP_K + [pltpu.SemaphoreType.DMA],
        compiler_params=_cparams(("arbitrary",), VMEM_LIMIT),
        name="combine",
    )(_slot_tiles(slot_t, tm), gates, x1, shared, ys, g, b)


def _filter_features(seq):
    t = np.linspace(0.0, 1.0, seq)[:, None]
    w = 2.0 * np.pi * np.arange(seq) / seq
    bands = np.linspace(1e-4, FILTER_BANDS - 1, FILTER_BANDS)
    ang = w[:, None] * bands[None, :]
    z = np.zeros((seq, FILTER_ORDER))
    z[:, :FILTER_EMB] = np.concatenate([t, np.cos(ang), -np.sin(ang)], axis=-1)
    return jnp.asarray(z, F32)


def _decay_rates():
    max_decay = math.log(DECAY_TARGET) / SHORT_DECAY_PCT
    min_decay = math.log(DECAY_TARGET) / LONG_DECAY_PCT
    return jnp.asarray(np.linspace(min_decay, max_decay, HYENA_WIDTH)[None, :], F32)


def kernel(x_prompt, x_sample, ln_in_g, ln_in_b, w_in, attn_sink, hy_conv_w, hy_conv_b, hy_f_w1, hy_f_b1,
           hy_f_freq, hy_f_w2, hy_f_b2, hy_f_w3, hy_skip, pool_w, pool_scale, out_norm_g, w_out, ln1_g, ln1_b,
           router_w, router_bias, exp_w_gate, exp_w_up, exp_w_down, sh_w_gate, sh_w_up, sh_w_down, ln2_g, ln2_b):
    bp, seq, d = x_prompt.shape
    bs = x_sample.shape[0]
    assert x_sample.shape[1] == seq and d == D_MODEL and seq % (FFT_N1 * SUBLANES) == 0
    nbatch = bp + bs
    t_prompt = bp * seq
    t = nbatch * seq
    n_blocks = t * TOP_K // MOE_MB + N_EXPERTS
    assert (t * TOP_K) % MOE_MB == 0

    plan = _FftPlan(seq)
    ctab, stab = _rope_tables(seq)
    w1p = jnp.pad(hy_f_w1, ((0, 0), (0, FILTER_ORDER - FILTER_EMB), (0, 0)))
    hspec = _hyena_filter_spectrum(plan, _filter_features(seq), w1p, hy_f_b1[:, None, :], hy_f_freq[:, None, :],
                                   hy_f_w2, hy_f_b2[:, None, :], hy_f_w3, _decay_rates())

    row = lambda v: v.reshape(1, -1)
    x, xb = _ln_in(x_prompt.reshape(t_prompt, d), x_sample.reshape(bs * seq, d), row(ln_in_g), row(ln_in_b))
    outs = None
    for l in range(DEPTH):
        proj3 = _in_proj(xb, w_in[l].astype(BF16)).reshape(nbatch, seq, IN_WIDTH)
        ya = _attention(proj3, attn_sink[l], ctab, stab)
        hsrc, hgate = _hyena_pre(proj3, hy_conv_w[l], row(hy_conv_b[l]))
        yh = _hyena(plan, hsrc, hgate, row(hy_skip[l]), hspec, l)
        yp = _pool(proj3, pool_w[l].astype(BF16), row(pool_scale[l]))
        x1, xpk, eidx_t, gate_t, rank_t, counts = _post_mixer(
            ya.reshape(t, ATTN_WIDTH), yh.reshape(t, HYENA_WIDTH), yp.reshape(t, POOL_WIDTH), x,
            row(out_norm_g[l]), w_out[l].astype(BF16), row(ln1_g[l]), row(ln1_b[l]), router_w[l], router_bias[l])
        slot_t, block_expert, n_used, last_block, has_rows = _dispatch_plan(eidx_t, rank_t, counts, n_blocks)
        xs, shared = _dispatch(xpk, slot_t, n_used, last_block, has_rows, sh_w_gate[l].astype(BF16),
                               sh_w_up[l].astype(BF16), sh_w_down[l].astype(BF16), n_blocks)
        ys = _experts(xs, block_expert, n_used, exp_w_gate, exp_w_up, exp_w_down, l, n_blocks)
        res = _combine(slot_t, gate_t.T, x1, shared, ys, row(ln2_g[l]), row(ln2_b[l]), t_prompt,
                       final=(l == DEPTH - 1))
        if l == DEPTH - 1:
            outs = res
        else:
            x, xb = res
    return outs[0].reshape(bp, seq, d), outs[1].reshape(bs, seq, d)
```

```python
import functools
import math

import jax
import jax.numpy as jnp
import numpy as np
from jax import lax
from jax.experimental import pallas as pl
from jax.experimental.pallas import tpu as pltpu

F32 = jnp.float32
BF16 = jnp.bfloat16
HIGHEST = lax.Precision.HIGHEST

D_MODEL = 2048
DEPTH = 2
HEAD_DIM = 128
ATTN_WIDTH = D_MODEL // 2
N_Q_HEADS = ATTN_WIDTH // HEAD_DIM
N_KV_HEADS = 2
Q_PER_KV = N_Q_HEADS // N_KV_HEADS
KV_WIDTH = N_KV_HEADS * HEAD_DIM
HYENA_WIDTH = D_MODEL // 4
POOL_WIDTH = D_MODEL - ATTN_WIDTH - HYENA_WIDTH
IN_WIDTH = ATTN_WIDTH + 2 * KV_WIDTH + 3 * HYENA_WIDTH + POOL_WIDTH
OFF_K = ATTN_WIDTH
OFF_V = OFF_K + KV_WIDTH
OFF_HY = OFF_V + KV_WIDTH
OFF_POOL = OFF_HY + 3 * HYENA_WIDTH

WINDOW = 128
BLOCK = 128
ROPE_THETA = 500000.0
ROPE_DIM = HEAD_DIM // 4
ROPE_HALF = ROPE_DIM // 2
NEG_BIG = -1e30
REMOVED = -3.0e38

SHORT_CONV = 3
FILTER_EMB = 33
FILTER_BANDS = (FILTER_EMB - 1) // 2
FILTER_ORDER = 64
DECAY_TARGET = 1e-2
SHORT_DECAY_PCT = 0.3
LONG_DECAY_PCT = 1.5

POOL_WINDOWS = (2, 4, 8, 16)
POOL_GROUP = POOL_WIDTH // len(POOL_WINDOWS)
OUT_GROUP = 128

N_EXPERTS = 64
TOP_K = 8
N_EXPERT_GROUPS = 8
EXPERTS_PER_GROUP = N_EXPERTS // N_EXPERT_GROUPS
TOPK_GROUPS = 4
EXPERT_HIDDEN = 512
SHARED_HIDDEN = 512
ROUTED_SCALE = 2.5

ALPHA = (2 * DEPTH) ** 0.25
LN_EPS = 1e-5
RMS_EPS = 1e-6

SUBLANES = 8
LANES = 128
VMEM_LIMIT = 56 * 1024 * 1024

FFT_N1 = 64
HY_CB = 128
MOE_MB = 512
HALF_D = D_MODEL // 2


def _cparams(sem, vmem=None):
    return pltpu.CompilerParams(dimension_semantics=sem, vmem_limit_bytes=vmem)


def _row_tile(t, pref):
    while t % pref:
        pref //= 2
    return pref


def _ln_rows(x, g, b):
    mu = jnp.mean(x, axis=-1, keepdims=True)
    xc = x - mu
    var = jnp.mean(xc * xc, axis=-1, keepdims=True)
    return xc * lax.rsqrt(var + LN_EPS) * g + b


def _pack_bf16_pair(x):
    h = x.shape[1] // 2
    hi = lax.bitcast_convert_type(x[:, :h].astype(BF16).astype(F32), jnp.uint32)
    lo = lax.bitcast_convert_type(x[:, h:].astype(BF16).astype(F32), jnp.uint32)
    return hi | (lo >> 16)


def _unpack_bf16_pair(pk):
    a = lax.bitcast_convert_type(pk & jnp.uint32(0xFFFF0000), F32).astype(BF16)
    b = lax.bitcast_convert_type(pk << 16, F32).astype(BF16)
    return a, b


def _ln_in_kernel(xp_ref, xs_ref, g_ref, b_ref, o_ref, ob_ref, *, n_prompt_blocks):
    i = pl.program_id(0)

    def emit(x):
        y = _ln_rows(x, g_ref[...], b_ref[...])
        o_ref[...] = y
        ob_ref[...] = y.astype(BF16)

    @pl.when(i < n_prompt_blocks)
    def _():
        emit(xp_ref[...])

    @pl.when(i >= n_prompt_blocks)
    def _():
        emit(xs_ref[...])


def _ln_in(xp, xs, g, b):
    tp, d = xp.shape
    ts = xs.shape[0]
    tm = _row_tile(math.gcd(tp, ts), 512)
    npb, nsb = tp // tm, ts // tm
    t = tp + ts
    return pl.pallas_call(
        functools.partial(_ln_in_kernel, n_prompt_blocks=npb),
        grid=(npb + nsb,),
        in_specs=[
            pl.BlockSpec((tm, d), lambda i: (jnp.minimum(i, npb - 1), 0)),
            pl.BlockSpec((tm, d), lambda i: (jnp.maximum(i - npb, 0), 0)),
            pl.BlockSpec((1, d), lambda i: (0, 0)),
            pl.BlockSpec((1, d), lambda i: (0, 0)),
        ],
        out_specs=[pl.BlockSpec((tm, d), lambda i: (i, 0)), pl.BlockSpec((tm, d), lambda i: (i, 0))],
        out_shape=[jax.ShapeDtypeStruct((t, d), F32), jax.ShapeDtypeStruct((t, d), BF16)],
        compiler_params=_cparams(("arbitrary",)),
        name="ln_in",
    )(xp, xs, g, b)


def _matmul_kernel(x_ref, w_ref, o_ref):
    o_ref[...] = jnp.dot(x_ref[...], w_ref[...], preferred_element_type=F32).astype(o_ref.dtype)


def _in_proj(xb, w):
    t, k = xb.shape
    n = w.shape[1]
    tm = _row_tile(t, 1024)
    tn = 512
    return pl.pallas_call(
        _matmul_kernel,
        grid=(t // tm, n // tn),
        in_specs=[pl.BlockSpec((tm, k), lambda i, j: (i, 0)), pl.BlockSpec((k, tn), lambda i, j: (0, j))],
        out_specs=pl.BlockSpec((tm, tn), lambda i, j: (i, j)),
        out_shape=jax.ShapeDtypeStruct((t, n), BF16),
        compiler_params=_cparams(("arbitrary", "arbitrary")),
        name="in_proj",
    )(xb, w)


def _rope_tables(seq):
    pos = np.arange(-BLOCK, seq + BLOCK, dtype=np.float64)
    inv = ROPE_THETA ** (-np.arange(ROPE_HALF, dtype=np.float64) / ROPE_HALF)
    ang = pos[:, None] * inv[None, :]
    c = np.ones((pos.shape[0], HEAD_DIM))
    s = np.zeros((pos.shape[0], HEAD_DIM))
    c[:, :ROPE_HALF] = np.cos(ang)
    c[:, ROPE_HALF:ROPE_DIM] = np.cos(ang)
    s[:, :ROPE_HALF] = -np.sin(ang)
    s[:, ROPE_HALF:ROPE_DIM] = np.sin(ang)
    return jnp.asarray(c, F32), jnp.asarray(s, F32)


def _rope(x, c, s):
    lane = lax.broadcasted_iota(jnp.int32, x.shape, 1)
    partner = jnp.where(lane < ROPE_HALF, pltpu.roll(x, HEAD_DIM - ROPE_HALF, 1), pltpu.roll(x, ROPE_HALF, 1))
    return x * c + partner * s


def _attn_kernel(sink_ref, q_ref, *refs, nb, qb):
    nkb = qb + 2
    k_refs, v_refs = refs[:nkb], refs[nkb:2 * nkb]
    ct_ref, st_ref, o_ref = refs[2 * nkb:]
    i = pl.program_id(1)
    base = pl.multiple_of(i * (qb * BLOCK), BLOCK)
    cw = ct_ref[pl.ds(base, nkb * BLOCK), :]
    sw = st_ref[pl.ds(base, nkb * BLOCK), :]
    kwin = jnp.concatenate([r[0] for r in k_refs], axis=0).astype(F32)
    vwin = jnp.concatenate([r[0] for r in v_refs], axis=0)
    q = q_ref[0].astype(F32)

    rows = Q_PER_KV * BLOCK
    rr = lax.broadcasted_iota(jnp.int32, (rows, 3 * BLOCK), 0) & (BLOCK - 1)
    cc = lax.broadcasted_iota(jnp.int32, (rows, 3 * BLOCK), 1)
    rel = cc - rr
    in_window = (rel >= BLOCK - WINDOW) & (rel <= BLOCK + WINDOW)
    rgrp = lax.broadcasted_iota(jnp.int32, (rows, 1), 0) // BLOCK

    for h in range(N_KV_HEADS):
        kh_all = _rope(kwin[:, h * HEAD_DIM:(h + 1) * HEAD_DIM], cw, sw).astype(BF16)
        vh_all = vwin[:, h * HEAD_DIM:(h + 1) * HEAD_DIM].astype(BF16)
        sink = jnp.zeros((rows, 1), F32)
        for g in range(Q_PER_KV):
            sink = jnp.where(rgrp == g, sink_ref[h * Q_PER_KV + g], sink)
        for sb in range(qb):
            blk = i * qb + sb
            lo = jnp.where(blk == 0, BLOCK, 0)
            hi = jnp.where(blk == nb - 1, 2 * BLOCK, 3 * BLOCK)
            valid = in_window & (cc >= lo) & (cc < hi)
            cq, sq = cw[(sb + 1) * BLOCK:(sb + 2) * BLOCK], sw[(sb + 1) * BLOCK:(sb + 2) * BLOCK]
            kh, vh = kh_all[sb * BLOCK:(sb + 3) * BLOCK], vh_all[sb * BLOCK:(sb + 3) * BLOCK]
            qs = []
            for g in range(Q_PER_KV):
                hq = h * Q_PER_KV + g
                qs.append(_rope(q[sb * BLOCK:(sb + 1) * BLOCK, hq * HEAD_DIM:(hq + 1) * HEAD_DIM], cq, sq).astype(BF16))
            qg = jnp.concatenate(qs, axis=0)
            s = lax.dot_general(qg, kh, (((1,), (1,)), ((), ())), preferred_element_type=F32) * (HEAD_DIM ** -0.5)
            s = jnp.where(valid, s, NEG_BIG)
            m = jnp.maximum(jnp.max(s, axis=-1, keepdims=True), sink)
            p = jnp.exp(s - m)
            denom = jnp.sum(p, axis=-1, keepdims=True) + jnp.exp(sink - m)
            o = jnp.dot((p / denom).astype(BF16), vh, preferred_element_type=F32)
            for g in range(Q_PER_KV):
                hq = h * Q_PER_KV + g
                o_ref[0, sb * BLOCK:(sb + 1) * BLOCK, hq * HEAD_DIM:(hq + 1) * HEAD_DIM] = o[g * BLOCK:(g + 1) * BLOCK]


def _attention(proj3, sink, ctab, stab):
    b, seq, _ = proj3.shape
    nb = seq // BLOCK
    qb = next(q for q in (4, 2, 1) if nb % q == 0)
    kcol, vcol = OFF_K // KV_WIDTH, OFF_V // KV_WIDTH

    def kv_spec(col, shift):
        return pl.BlockSpec((1, BLOCK, KV_WIDTH), lambda bi, i: (bi, jnp.clip(i * qb + shift, 0, nb - 1), col))

    shifts = range(-1, qb + 1)
    n_kv = 2 * len(shifts)
    return pl.pallas_call(
        functools.partial(_attn_kernel, nb=nb, qb=qb),
        grid=(b, nb // qb),
        in_specs=[pl.BlockSpec(memory_space=pltpu.SMEM),
                  pl.BlockSpec((1, qb * BLOCK, ATTN_WIDTH), lambda bi, i: (bi, i, 0))]
                 + [kv_spec(kcol, s) for s in shifts] + [kv_spec(vcol, s) for s in shifts]
                 + [pl.BlockSpec(ctab.shape, lambda bi, i: (0, 0)), pl.BlockSpec(stab.shape, lambda bi, i: (0, 0))],
        out_specs=pl.BlockSpec((1, qb * BLOCK, ATTN_WIDTH), lambda bi, i: (bi, i, 0)),
        out_shape=jax.ShapeDtypeStruct((b, seq, ATTN_WIDTH), F32),
        compiler_params=_cparams(("arbitrary", "arbitrary")),
        name="attention",
    )(sink, *([proj3] * (1 + n_kv)), ctab, stab)


class _FftPlan:
    def __init__(self, seq):
        self.seq = seq
        self.n = 2 * seq
        self.n1 = FFT_N1
        self.n2 = self.n // self.n1
        self.nh = seq // self.n1
        self.k2 = self.n2 // 2 + 1
        self.k2p = -(-self.k2 // SUBLANES) * SUBLANES
        self.srow = 2 * self.n1 + SUBLANES
        n1 = np.arange(self.n1)
        n2 = np.arange(self.nh)
        k2 = np.arange(self.k2)
        npos = self.n1 * n2[None, None, :] + n1[:, None, None]
        ang = 2.0 * np.pi * k2[None, :, None] * npos / self.n
        f1 = np.zeros((self.n1, 2 * self.k2p, self.nh))
        f1[:, :self.k2] = np.cos(ang)
        f1[:, self.k2p:self.k2p + self.k2] = -np.sin(ang)
        wk = np.full((self.k2,), 2.0)
        wk[0] = 1.0
        wk[-1] = 1.0
        gm = np.zeros((self.n1, self.nh, 2 * self.k2p))
        angt = np.transpose(ang, (0, 2, 1))
        gm[:, :, :self.k2] = np.cos(angt) * wk / self.n
        gm[:, :, self.k2p:self.k2p + self.k2] = -np.sin(angt) * wk / self.n
        a2 = 2.0 * np.pi * np.outer(n1, n1) / self.n1
        c2, s2 = np.cos(a2), np.sin(a2)
        self.f1 = _split3_lhs(f1)
        self.g = _split3_lhs(gm)
        self.m2 = _split3_lhs(np.block([[c2, s2], [-s2, c2]]))
        self.m2i = _split3_lhs(np.block([[c2, -s2], [s2, c2]]))


def _dotf(a, b):
    return jnp.dot(a, b, precision=HIGHEST, preferred_element_type=F32)


def _split3_lhs(m):
    m32 = np.asarray(m, np.float32)
    hi = m32.astype(BF16)
    lo = (m32 - hi.astype(np.float32)).astype(BF16)
    return jnp.asarray(np.concatenate([hi, hi, lo], axis=-1))


def _split3_rhs(x):
    hi = x.astype(BF16)
    lo = (x - hi.astype(F32)).astype(BF16)
    return jnp.concatenate([hi, lo, hi], axis=0)


def _dot3(lhs3, x):
    return jnp.dot(lhs3, _split3_rhs(x), preferred_element_type=F32)


def _lane_cat(parts):
    return parts[0] if len(parts) == 1 else jnp.concatenate(parts, axis=1)


def _fft_stage1(plan, load_rows, f1_ref, slab_refs):
    unroll = 4

    def body(grp, carry):
        for u in range(unroll):
            n1 = grp * unroll + u
            res = _dot3(f1_ref[n1], _lane_cat([ld(n1) for ld in load_rows]))
            for p, slab in enumerate(slab_refs):
                part = res[:, p * LANES:(p + 1) * LANES]
                slab[pl.ds(n1, plan.k2p, stride=plan.srow), :] = part[:plan.k2p]
                slab[pl.ds(plan.n1 + n1, plan.k2p, stride=plan.srow), :] = part[plan.k2p:]
        return carry

    lax.fori_loop(0, plan.n1 // unroll, body, 0)


def _slab_rows(plan, slab_refs, k2):
    off = pl.multiple_of(k2 * plan.srow, SUBLANES)
    return off, _lane_cat([s[pl.ds(off, 2 * plan.n1), :] for s in slab_refs])


def _hyena_filter_kernel(z_ref, w1_ref, b1_ref, fr_ref, w2_ref, b2_ref, w3f_ref, w3b_ref, dl_ref,
                         f1_ref, m2_ref, h_ref, srcf_ref, srcb_ref, slabf_ref, slabb_ref, hid_ref, *, plan):
    seq = plan.seq

    @pl.when(pl.program_id(1) == 0)
    def _():
        fr = fr_ref[0]
        h1 = jnp.sin(fr * (_dotf(z_ref[...], w1_ref[0]) + b1_ref[0]))
        hid_ref[...] = jnp.sin(fr * (_dotf(h1, w2_ref[0]) + b2_ref[0]))

    hid = hid_ref[...]
    row = lax.broadcasted_iota(jnp.int32, (seq, 1), 0)
    t = row.astype(F32) * (1.0 / (seq - 1))
    decay = jnp.exp(-t * jnp.abs(dl_ref[...]))
    srcf_ref[...] = _dotf(hid, w3f_ref[0]) * decay
    srcb_ref[...] = jnp.where(row == 0, 0.0, _dotf(hid, w3b_ref[0]) * decay)
    slabs = (slabf_ref, slabb_ref)
    _fft_stage1(plan, [lambda n1, r=r: r[pl.ds(n1, plan.nh, stride=plan.n1), :] for r in (srcf_ref, srcb_ref)],
                f1_ref, slabs)

    unroll = 5 if plan.k2 % 5 == 0 else 1

    def body(grp, carry):
        for u in range(unroll):
            k2 = grp * unroll + u
            _, a = _slab_rows(plan, slabs, k2)
            x = _dot3(m2_ref[...], a)
            h_ref[0, k2, :plan.n1, :] = x[:plan.n1, :LANES] + x[:plan.n1, LANES:]
            h_ref[0, k2, plan.n1:, :] = x[plan.n1:, :LANES] - x[plan.n1:, LANES:]
        return carry

    lax.fori_loop(0, plan.k2 // unroll, body, 0)


def _hyena_filter_spectrum(plan, z, w1p, b1, freq, w2, b2, w3, deltas):
    depth = w3.shape[0]
    cb = HY_CB
    ncb = HYENA_WIDTH // cb
    seq = plan.seq
    lay = lambda shp: pl.BlockSpec((1,) + shp, lambda l, j: (l,) + (0,) * len(shp))
    return pl.pallas_call(
        functools.partial(_hyena_filter_kernel, plan=plan),
        grid=(depth, ncb),
        in_specs=[
            pl.BlockSpec(z.shape, lambda l, j: (0, 0)),
            lay(w1p.shape[1:]), lay(b1.shape[1:]), lay(freq.shape[1:]), lay(w2.shape[1:]), lay(b2.shape[1:]),
            pl.BlockSpec((1, FILTER_ORDER, cb), lambda l, j: (l, 0, j)),
            pl.BlockSpec((1, FILTER_ORDER, cb), lambda l, j: (l, 0, ncb + j)),
            pl.BlockSpec((1, cb), lambda l, j: (0, j)),
            pl.BlockSpec(plan.f1.shape, lambda l, j: (0, 0, 0)),
            pl.BlockSpec(plan.m2.shape, lambda l, j: (0, 0)),
        ],
        out_specs=pl.BlockSpec((1, plan.k2, 2 * plan.n1, cb), lambda l, j: (l, 0, 0, j)),
        out_shape=jax.ShapeDtypeStruct((depth, plan.k2, 2 * plan.n1, HYENA_WIDTH), F32),
        scratch_shapes=[pltpu.VMEM((seq, cb), F32), pltpu.VMEM((seq, cb), F32),
                        pltpu.VMEM((plan.k2p * plan.srow, cb), F32), pltpu.VMEM((plan.k2p * plan.srow, cb), F32),
                        pltpu.VMEM((seq, FILTER_ORDER), F32)],
        compiler_params=_cparams(("arbitrary", "arbitrary"), VMEM_LIMIT),
        name="hyena_filter",
    )(z, w1p, b1, freq, w2, b2, w3, w3, deltas, plan.f1, plan.m2)


def _short_conv(u, w, b, seq):
    row = lax.broadcasted_iota(jnp.int32, (seq, 1), 0)
    prev = jnp.where(row >= 1, pltpu.roll(u, 1, 0), 0.0)
    nxt = jnp.where(row < seq - 1, pltpu.roll(u, seq - 1, 0), 0.0)
    return prev * w[0:1] + u * w[1:2] + nxt * w[2:3] + b


def _hyena_pre_kernel(x0_ref, x1_ref, v_ref, w0_ref, w1_ref, wv_ref, b0_ref, b1_ref, bv_ref, src_ref, gate_ref, *, seq):
    gate_ref[0] = _short_conv(x0_ref[0].astype(F32), w0_ref[...], b0_ref[...], seq)
    src_ref[0] = (_short_conv(v_ref[0].astype(F32), wv_ref[...], bv_ref[...], seq)
                  * _short_conv(x1_ref[0].astype(F32), w1_ref[...], b1_ref[...], seq))


def _hyena_pre(proj3, conv_w, conv_b):
    b, seq, _ = proj3.shape
    cb = HY_CB
    ncb = HYENA_WIDTH // cb
    c0 = OFF_HY // cb

    def u_spec(part):
        return pl.BlockSpec((1, seq, cb), lambda bi, j: (bi, 0, c0 + part * ncb + j))

    def w_spec(part):
        return pl.BlockSpec((SHORT_CONV, cb), lambda bi, j: (0, part * ncb + j))

    def b_spec(part):
        return pl.BlockSpec((1, cb), lambda bi, j: (0, part * ncb + j))

    out = pl.BlockSpec((1, seq, cb), lambda bi, j: (bi, 0, j))
    shp = jax.ShapeDtypeStruct((b, seq, HYENA_WIDTH), F32)
    return pl.pallas_call(
        functools.partial(_hyena_pre_kernel, seq=seq),
        grid=(b, ncb),
        in_specs=[u_spec(0), u_spec(1), u_spec(2), w_spec(0), w_spec(1), w_spec(2), b_spec(0), b_spec(1), b_spec(2)],
        out_specs=[out, out],
        out_shape=[shp, shp],
        compiler_params=_cparams(("arbitrary", "arbitrary"), VMEM_LIMIT),
        name="hyena_pre",
    )(proj3, proj3, proj3, conv_w, conv_w, conv_w, conv_b, conv_b, conv_b)


def _hyena_kernel(src_ref, gate_ref, skip_ref, h_ref, f1_ref, m2_ref, m2i_ref, g_ref, o_ref, *slab_refs, plan):
    npar = len(slab_refs)
    n1c = plan.n1
    _fft_stage1(plan, [lambda n1, p=p: src_ref[p, pl.ds(n1, plan.nh, stride=n1c), :] for p in range(npar)],
                f1_ref, slab_refs)

    unroll2 = 5 if plan.k2 % 5 == 0 else 1

    def stage2(grp, carry):
        for u in range(unroll2):
            k2 = grp * unroll2 + u
            off, a = _slab_rows(plan, slab_refs, k2)
            x = _dot3(m2_ref[...], a)
            h = _lane_cat([h_ref[0, k2]] * npar)
            xr, xi = x[:n1c], x[n1c:]
            hr, hi = h[:n1c], h[n1c:]
            y = jnp.concatenate([xr * hr - xi * hi, xr * hi + xi * hr], axis=0)
            back = _dot3(m2i_ref[...], y)
            for p, slab in enumerate(slab_refs):
                slab[pl.ds(off, 2 * n1c), :] = back[:, p * LANES:(p + 1) * LANES]
        return carry

    lax.fori_loop(0, plan.k2 // unroll2, stage2, 0)

    unroll3 = 4

    def stage3(grp, carry):
        for u in range(unroll3):
            n1 = grp * unroll3 + u
            a = _lane_cat([jnp.concatenate([s[pl.ds(n1, plan.k2p, stride=plan.srow), :],
                                            s[pl.ds(n1c + n1, plan.k2p, stride=plan.srow), :]], axis=0)
                           for s in slab_refs])
            y = _dot3(g_ref[n1], a)
            for p in range(npar):
                o_ref[p, pl.ds(n1, plan.nh, stride=n1c), :] = y[:, p * LANES:(p + 1) * LANES]
        return carry

    lax.fori_loop(0, n1c // unroll3, stage3, 0)
    for p in range(npar):
        o_ref[p] = (o_ref[p] + src_ref[p] * skip_ref[...]) * gate_ref[p]


def _hyena(plan, src, gate, skip, hspec, layer):
    b, seq, _ = src.shape
    cb = HY_CB
    ncb = HYENA_WIDTH // cb
    npar = 2 if b % 2 == 0 else 1
    once = pl.Buffered(1)
    const = lambda a: pl.BlockSpec(a.shape, lambda j, bi: (0,) * a.ndim, pipeline_mode=once)
    seq_spec = pl.BlockSpec((npar, seq, cb), lambda j, bi: (bi, 0, j))
    return pl.pallas_call(
        functools.partial(_hyena_kernel, plan=plan),
        grid=(ncb, b // npar),
        in_specs=[
            seq_spec, seq_spec,
            pl.BlockSpec((1, cb), lambda j, bi: (0, j)),
            pl.BlockSpec((1, plan.k2, 2 * plan.n1, cb), lambda j, bi: (layer, 0, 0, j), pipeline_mode=once),
            const(plan.f1), const(plan.m2), const(plan.m2i), const(plan.g),
        ],
        out_specs=seq_spec,
        out_shape=jax.ShapeDtypeStruct((b, seq, HYENA_WIDTH), F32),
        scratch_shapes=[pltpu.VMEM((plan.k2p * plan.srow, cb), F32)] * npar,
        compiler_params=_cparams(("arbitrary", "arbitrary"), VMEM_LIMIT),
        name="hyena",
    )(src, gate, skip, hspec, plan.f1, plan.m2, plan.m2i, plan.g)


def _pool_kernel(u_ref, w_ref, sc_ref, o_ref, *, seq):
    row = lax.broadcasted_iota(jnp.int32, (seq, 1), 0)

    def back(x, s):
        return jnp.where(row >= s, pltpu.roll(x, s, 0), 0.0)

    def fwd(x, s):
        return jnp.where(row < seq - s, pltpu.roll(x, seq - s, 0), 0.0)

    for gi, win in enumerate(POOL_WINDOWS):
        half = win // 2
        u = u_ref[0, :, gi * POOL_GROUP:(gi + 1) * POOL_GROUP].astype(F32)
        ahead, behind, s = u, u, 1
        while s < half:
            ahead = ahead + fwd(ahead, s)
            behind = behind + back(behind, s)
            s *= 2
        total = ahead + back(behind, 1)
        cnt = (jnp.minimum(row + half, seq) - jnp.maximum(row - half, 0)).astype(F32)
        diff = total / cnt - u
        y = jnp.dot(diff.astype(BF16), w_ref[gi], preferred_element_type=F32)
        o_ref[0, :, gi * POOL_GROUP:(gi + 1) * POOL_GROUP] = y * sc_ref[:, gi * POOL_GROUP:(gi + 1) * POOL_GROUP]


def _pool(proj3, pool_w, pool_scale):
    b, seq, _ = proj3.shape
    return pl.pallas_call(
        functools.partial(_pool_kernel, seq=seq),
        grid=(b,),
        in_specs=[
            pl.BlockSpec((1, seq, POOL_WIDTH), lambda bi: (bi, 0, OFF_POOL // POOL_WIDTH)),
            pl.BlockSpec(pool_w.shape, lambda bi: (0, 0, 0)),
            pl.BlockSpec((1, POOL_WIDTH), lambda bi: (0, 0)),
        ],
        out_specs=pl.BlockSpec((1, seq, POOL_WIDTH), lambda bi: (bi, 0, 0)),
        out_shape=jax.ShapeDtypeStruct((b, seq, POOL_WIDTH), F32),
        compiler_params=_cparams(("arbitrary",), VMEM_LIMIT),
        name="pool",
    )(proj3, pool_w, pool_scale)


def _post_mixer_kernel(ya_ref, yh_ref, yp_ref, x_ref, gn_ref, w_ref, g_ref, b_ref, rw_ref, bias_ref, tri_ref,
                       o_ref, opk_ref, eidx_ref, gate_ref, rank_ref, cnt_ref, carry_ref):
    parts = []
    for src in (ya_ref, yh_ref, yp_ref):
        for j in range(src.shape[1] // OUT_GROUP):
            c = src[:, j * OUT_GROUP:(j + 1) * OUT_GROUP]
            parts.append(c * lax.rsqrt(jnp.mean(c * c, axis=-1, keepdims=True) + RMS_EPS))
    yn = (jnp.concatenate(parts, axis=-1) * gn_ref[...]).astype(BF16)
    mix = jnp.dot(yn, w_ref[...], preferred_element_type=F32)
    x1 = _ln_rows(ALPHA * x_ref[...] + mix, g_ref[...], b_ref[...])
    o_ref[...] = x1
    opk_ref[...] = _pack_bf16_pair(x1)
    _route(x1, rw_ref, bias_ref, tri_ref, eidx_ref, gate_ref, rank_ref, cnt_ref, carry_ref)


def _post_mixer(ya, yh, yp, x, gn, w_out, g, b, router_w, router_bias):
    t, d = x.shape
    tm = _row_tile(t, 512)
    row = lambda w: pl.BlockSpec((tm, w), lambda i: (i, 0))
    vec = pl.BlockSpec((1, d), lambda i: (0, 0))
    once = pl.Buffered(1)
    const = lambda a: pl.BlockSpec(a.shape, lambda i: (0,) * a.ndim, pipeline_mode=once)
    rw3 = _split3_lhs_traced(router_w.T)
    bias_b = jnp.broadcast_to(router_bias[:, None], (N_EXPERTS, tm)).astype(F32)
    tri = jnp.asarray(np.triu(np.ones((tm, tm)), 1), BF16)
    kt = lambda dt: jax.ShapeDtypeStruct((TOP_K, t), dt)
    kspec = pl.BlockSpec((TOP_K, tm), lambda i: (0, i))
    return pl.pallas_call(
        _post_mixer_kernel,
        grid=(t // tm,),
        in_specs=[row(ATTN_WIDTH), row(HYENA_WIDTH), row(POOL_WIDTH), row(d), vec, const(w_out), vec, vec,
                  const(rw3), const(bias_b), const(tri)],
        out_specs=[row(d), row(d // 2), kspec, kspec, kspec, pl.BlockSpec((N_EXPERTS, LANES), lambda i: (0, 0))],
        out_shape=[jax.ShapeDtypeStruct((t, d), F32), jax.ShapeDtypeStruct((t, d // 2), jnp.uint32),
                   kt(jnp.int32), kt(F32), kt(jnp.int32), jax.ShapeDtypeStruct((N_EXPERTS, LANES), F32)],
        scratch_shapes=[pltpu.VMEM((N_EXPERTS, LANES), F32)],
        compiler_params=_cparams(("arbitrary",), VMEM_LIMIT),
        name="post_mixer",
    )(ya, yh, yp, x, gn, w_out, g, b, rw3, bias_b, tri)


def _split3_lhs_traced(m):
    hi = m.astype(BF16)
    lo = (m - hi.astype(F32)).astype(BF16)
    return jnp.concatenate([hi, hi, lo], axis=-1)


def _route(x1, rw_ref, bias_ref, tri_ref, eidx_ref, gate_ref, rank_ref, cnt_ref, carry_ref):
    tm = x1.shape[0]

    @pl.when(pl.program_id(0) == 0)
    def _():
        carry_ref[...] = jnp.zeros_like(carry_ref)

    xh = x1.astype(BF16)
    xl = (x1 - xh.astype(F32)).astype(BF16)
    logits = lax.dot_general(rw_ref[...], jnp.concatenate([xh, xl, xh], axis=1), (((1,), (1,)), ((), ())),
                             preferred_element_type=F32)
    scores = jax.nn.sigmoid(logits)
    choice = scores + bias_ref[...]
    sub = lax.broadcasted_iota(jnp.int32, (EXPERTS_PER_GROUP, tm), 0)
    far = jnp.int32(N_EXPERTS)

    def first_argmax(v):
        m = jnp.max(v, axis=0, keepdims=True)
        return m, jnp.min(jnp.where(v == m, sub, far), axis=0, keepdims=True)

    tiles, stiles, gscore = [], [], []
    for g in range(N_EXPERT_GROUPS):
        c = choice[g * EXPERTS_PER_GROUP:(g + 1) * EXPERTS_PER_GROUP]
        tiles.append(c)
        stiles.append(scores[g * EXPERTS_PER_GROUP:(g + 1) * EXPERTS_PER_GROUP])
        m1, i1 = first_argmax(c)
        m2 = jnp.max(jnp.where(sub == i1, REMOVED, c), axis=0, keepdims=True)
        gscore.append(m1 + m2)
    cur = jnp.concatenate(gscore, axis=0)
    gsel = jnp.zeros(cur.shape, F32)
    for _ in range(TOPK_GROUPS):
        _, ig = first_argmax(cur)
        hit = sub == ig
        gsel = jnp.where(hit, 1.0, gsel)
        cur = jnp.where(hit, REMOVED, cur)

    eids = [sub + g * EXPERTS_PER_GROUP for g in range(N_EXPERT_GROUPS)]
    masked = [jnp.where(gsel[g:g + 1] > 0.5, tiles[g], NEG_BIG) for g in range(N_EXPERT_GROUPS)]
    picked = [jnp.zeros((EXPERTS_PER_GROUP, tm), F32) for _ in range(N_EXPERT_GROUPS)]
    idxs, sels = [], []
    for _ in range(TOP_K):
        m = functools.reduce(jnp.maximum, [jnp.max(v, axis=0, keepdims=True) for v in masked])
        idx = functools.reduce(jnp.minimum, [jnp.min(jnp.where(v == m, e, far), axis=0, keepdims=True)
                                             for v, e in zip(masked, eids)])
        sc = jnp.zeros((1, tm), F32)
        for g in range(N_EXPERT_GROUPS):
            hit = eids[g] == idx
            sc = sc + jnp.sum(jnp.where(hit, stiles[g], 0.0), axis=0, keepdims=True)
            masked[g] = jnp.where(hit, REMOVED, masked[g])
            picked[g] = jnp.where(hit, 1.0, picked[g])
        idxs.append(idx)
        sels.append(sc)
    total = functools.reduce(lambda a, c: a + c, sels)
    for k in range(TOP_K):
        eidx_ref[k:k + 1, :] = idxs[k]
        gate_ref[k:k + 1, :] = sels[k] / total * ROUTED_SCALE

    sel = jnp.concatenate(picked, axis=0)
    before = jnp.dot(sel.astype(BF16), tri_ref[...], preferred_element_type=F32) + carry_ref[:, 0:1]
    for k in range(TOP_K):
        r = jnp.zeros((1, tm), F32)
        for g in range(N_EXPERT_GROUPS):
            r = r + jnp.sum(jnp.where(eids[g] == idxs[k], before[g * EXPERTS_PER_GROUP:(g + 1) * EXPERTS_PER_GROUP], 0.0),
                            axis=0, keepdims=True)
        rank_ref[k:k + 1, :] = r.astype(jnp.int32)
    carry_ref[...] = carry_ref[...] + jnp.sum(sel, axis=1, keepdims=True)
    cnt_ref[...] = carry_ref[...]


def _dispatch_plan(eidx_t, rank_t, counts, n_blocks):
    mb = MOE_MB
    cnt = counts[:, 0].astype(jnp.int32)
    padded = (cnt + mb - 1) // mb * mb
    pend = jnp.cumsum(padded)
    pstart = pend - padded
    experts = jnp.arange(N_EXPERTS, dtype=jnp.int32)
    slot_t = rank_t + jnp.sum(jnp.where(eidx_t[:, :, None] == experts, pstart, 0), axis=-1)
    block_end = pend // mb
    blocks = jnp.arange(n_blocks, dtype=jnp.int32)
    block_expert = jnp.minimum(jnp.sum((block_end[None, :] <= blocks[:, None]).astype(jnp.int32), axis=1),
                               N_EXPERTS - 1).astype(jnp.int32)
    n_used = block_end[-1:].astype(jnp.int32)
    last_block = (block_end - 1).astype(jnp.int32)
    has_rows = (padded > 0).astype(jnp.int32)
    return slot_t.astype(jnp.int32), block_expert, n_used, last_block, has_rows


def _dispatch_kernel(last_ref, has_ref, nused_ref, slot_ref, x_ref, wg_ref, wu_ref, wd_ref, xs_ref, sh_ref,
                     zero_ref, zsem, sem, *, n_blocks):
    mb = MOE_MB
    tm = x_ref.shape[0]

    def zero_copy(block):
        return pltpu.make_async_copy(zero_ref, xs_ref.at[pl.ds(pl.multiple_of(block * mb, mb), mb)], zsem)

    @pl.when(pl.program_id(0) == 0)
    def _():
        zero_ref[...] = jnp.zeros_like(zero_ref)

        def start_e(e, c):
            @pl.when(has_ref[e] > 0)
            def _():
                zero_copy(last_ref[e]).start()
            return c

        def wait_e(e, c):
            @pl.when(has_ref[e] > 0)
            def _():
                zero_copy(last_ref[e]).wait()
            return c

        def start_b(blk, c):
            zero_copy(blk).start()
            return c

        def wait_b(blk, c):
            zero_copy(blk).wait()
            return c

        lax.fori_loop(0, N_EXPERTS, start_e, 0)
        lax.fori_loop(nused_ref[0], n_blocks, start_b, 0)
        lax.fori_loop(0, N_EXPERTS, wait_e, 0)
        lax.fori_loop(nused_ref[0], n_blocks, wait_b, 0)

    groups = list(range(tm // SUBLANES))

    def scatter_rows(n_groups):
        for grp in groups[:n_groups]:
            for j in range(SUBLANES):
                r = grp * SUBLANES + j
                for k in range(TOP_K):
                    slot = slot_ref[0, r * TOP_K + k]
                    pltpu.make_async_copy(x_ref.at[pl.ds(r, 1)], xs_ref.at[pl.ds(slot, 1)], sem).start(priority=k % 2)
        del groups[:n_groups]

    chunk = 2 * LANES
    n_pieces = 2 * (SHARED_HIDDEN // chunk) + D_MODEL // chunk
    per_piece = -(-len(groups) // n_pieces)
    a, b = _unpack_bf16_pair(x_ref[...])
    pre = []
    for w_ref in (wg_ref, wu_ref):
        cols = []
        for c in range(SHARED_HIDDEN // chunk):
            scatter_rows(per_piece)
            cs = slice(c * chunk, (c + 1) * chunk)
            cols.append(jnp.dot(a, w_ref[:HALF_D, cs], preferred_element_type=F32)
                        + jnp.dot(b, w_ref[HALF_D:, cs], preferred_element_type=F32))
        pre.append(jnp.concatenate(cols, axis=1))
    hid = (pre[0] * jax.nn.sigmoid(pre[0]) * pre[1]).astype(BF16)
    for c in range(D_MODEL // chunk):
        scatter_rows(per_piece)
        cs = slice(c * chunk, (c + 1) * chunk)
        sh_ref[:, cs] = jnp.dot(hid, wd_ref[:, cs], preferred_element_type=F32)
    scatter_rows(len(groups))
    for k in range(TOP_K):
        pltpu.make_async_copy(x_ref, xs_ref.at[pl.ds(0, tm)], sem).wait()


def _slot_tiles(slot_t, tm):
    t = slot_t.shape[1]
    return slot_t.T.reshape(t // tm, 1, tm * TOP_K)


def _dispatch(xpk, slot_t, n_used, last_block, has_rows, wg, wu, wd, n_blocks):
    t, hw = xpk.shape
    tm = _row_tile(t, 256)
    full = lambda a: pl.BlockSpec(a.shape, lambda i, *_: (0,) * a.ndim)
    grid_spec = pltpu.PrefetchScalarGridSpec(
        num_scalar_prefetch=3,
        grid=(t // tm,),
        in_specs=[
            pl.BlockSpec((None, 1, tm * TOP_K), lambda i, *_: (i, 0, 0), memory_space=pltpu.SMEM),
            pl.BlockSpec((tm, hw), lambda i, *_: (i, 0)),
            full(wg), full(wu), full(wd),
        ],
        out_specs=[pl.BlockSpec(memory_space=pl.ANY), pl.BlockSpec((tm, D_MODEL), lambda i, *_: (i, 0))],
        scratch_shapes=[pltpu.VMEM((MOE_MB, hw), jnp.uint32), pltpu.SemaphoreType.DMA, pltpu.SemaphoreType.DMA],
    )
    return pl.pallas_call(
        functools.partial(_dispatch_kernel, n_blocks=n_blocks),
        grid_spec=grid_spec,
        out_shape=[jax.ShapeDtypeStruct((n_blocks * MOE_MB, hw), jnp.uint32), jax.ShapeDtypeStruct((t, D_MODEL), F32)],
        compiler_params=_cparams(("arbitrary",), VMEM_LIMIT),
        name="dispatch",
    )(last_block, has_rows, n_used, _slot_tiles(slot_t, tm), xpk, wg, wu, wd)


def _swiglu_packed(pk, wg, wu, wd):
    a, b = _unpack_bf16_pair(pk)
    gate = jnp.dot(a, wg[:HALF_D], preferred_element_type=F32) + jnp.dot(b, wg[HALF_D:], preferred_element_type=F32)
    up = jnp.dot(a, wu[:HALF_D], preferred_element_type=F32) + jnp.dot(b, wu[HALF_D:], preferred_element_type=F32)
    hid = (gate * jax.nn.sigmoid(gate) * up).astype(BF16)
    return jnp.dot(hid, wd, preferred_element_type=F32)


def _experts_kernel(be_ref, nused_ref, nxt_ref, xs_ref, wg_hbm, wu_hbm, wd_hbm, y_ref,
                    sg_ref, su_ref, sd_ref, wgb_ref, wub_ref, wdb_ref, sems, grp_ref, *, layer):
    i = pl.program_id(0)
    e = be_ref[i]
    first = (i == 0) | (e != be_ref[jnp.maximum(i - 1, 0)])

    def fetch(expert, slot):
        pairs = ((wg_hbm, sg_ref), (wu_hbm, su_ref), (wd_hbm, sd_ref))
        return [pltpu.make_async_copy(w.at[layer, expert], s.at[slot], sems.at[slot, j])
                for j, (w, s) in enumerate(pairs)]

    @pl.when(i == 0)
    def _():
        grp_ref[0] = 0
        for c in fetch(e, 0):
            c.start()

    @pl.when(first)
    def _():
        slot = grp_ref[0] & 1
        for c in fetch(e, slot):
            c.wait()
        nxt = nxt_ref[i]

        @pl.when(nxt != e)
        def _():
            for c in fetch(nxt, 1 - slot):
                c.start()

        wgb_ref[...] = sg_ref[slot].astype(BF16)
        wub_ref[...] = su_ref[slot].astype(BF16)
        wdb_ref[...] = sd_ref[slot].astype(BF16)
        grp_ref[0] = grp_ref[0] + 1

    @pl.when(i < nused_ref[0])
    def _():
        y_ref[...] = _swiglu_packed(xs_ref[...], wgb_ref, wub_ref, wdb_ref[...])

    @pl.when(i >= nused_ref[0])
    def _():
        y_ref[...] = jnp.zeros_like(y_ref)


def _next_expert(block_expert):
    experts = jnp.arange(N_EXPERTS, dtype=jnp.int32)
    present = jnp.any(block_expert[:, None] == experts[None, :], axis=0)
    later = jnp.where((experts[None, :] > block_expert[:, None]) & present[None, :], experts[None, :], N_EXPERTS)
    nxt = jnp.min(later, axis=1)
    return jnp.where(nxt == N_EXPERTS, block_expert, nxt).astype(jnp.int32)


def _experts(xs, block_expert, n_used, wg, wu, wd, layer, n_blocks):
    hw = xs.shape[1]
    d = wd.shape[3]
    hbm = pl.BlockSpec(memory_space=pl.ANY)
    grid_spec = pltpu.PrefetchScalarGridSpec(
        num_scalar_prefetch=3,
        grid=(n_blocks,),
        in_specs=[pl.BlockSpec((MOE_MB, hw), lambda i, *_: (i, 0)), hbm, hbm, hbm],
        out_specs=pl.BlockSpec((MOE_MB, d), lambda i, *_: (i, 0)),
        scratch_shapes=[pltpu.VMEM((2,) + wg.shape[2:], F32), pltpu.VMEM((2,) + wu.shape[2:], F32),
                        pltpu.VMEM((2,) + wd.shape[2:], F32),
                        pltpu.VMEM(wg.shape[2:], BF16), pltpu.VMEM(wu.shape[2:], BF16), pltpu.VMEM(wd.shape[2:], BF16),
                        pltpu.SemaphoreType.DMA((2, 3)), pltpu.SMEM((1,), jnp.int32)],
    )
    return pl.pallas_call(
        functools.partial(_experts_kernel, layer=layer),
        grid_spec=grid_spec,
        out_shape=jax.ShapeDtypeStruct((n_blocks * MOE_MB, d), F32),
        compiler_params=_cparams(("arbitrary",), VMEM_LIMIT),
        name="experts",
    )(block_expert, n_used, _next_expert(block_expert), xs, wg, wu, wd)


def _combine_kernel(slot_ref, slot_next_ref, gates_ref, x_ref, sh_ref, ys_ref, g_ref, b_ref, *rest,
                    n_prompt_steps, n_steps, final):
    outs = rest[:2]
    sets = (rest[2:2 + TOP_K], rest[2 + TOP_K:2 + 2 * TOP_K])
    sems = rest[2 + 2 * TOP_K]
    tm = x_ref.shape[0] // 2
    d = x_ref.shape[1]
    ngroups = tm // SUBLANES
    i = pl.program_id(0)

    def gather(table_ref, first_row, which):
        def body(grp, c):
            for j in range(SUBLANES):
                for k in range(TOP_K):
                    slot = table_ref[0, (first_row + grp * SUBLANES + j) * TOP_K + k]
                    pltpu.make_async_copy(ys_ref.at[pl.ds(slot, 1)], sets[which][k].at[grp, pl.ds(j, 1)],
                                          sems.at[which]).start(priority=k % 2)
            return c

        lax.fori_loop(0, ngroups, body, 0)

    def drain(which):
        def body(grp, c):
            for k in range(TOP_K):
                pltpu.make_async_copy(ys_ref.at[pl.ds(0, SUBLANES)], sets[which][k].at[grp], sems.at[which]).wait()
            return c

        lax.fori_loop(0, ngroups, body, 0)

    def reduce_tile(half, which):
        rows = slice(half * tm, (half + 1) * tm)
        gates = gates_ref[rows, :]
        routed = gates[:, 0:1] * sets[which][0][...].reshape(tm, d)
        for k in range(1, TOP_K):
            routed = routed + gates[:, k:k + 1] * sets[which][k][...].reshape(tm, d)
        y = _ln_rows(ALPHA * x_ref[rows, :] + (routed + sh_ref[rows, :]), g_ref[...], b_ref[...])
        if final:
            @pl.when(i < n_prompt_steps)
            def _():
                outs[0][rows, :] = y

            @pl.when(i >= n_prompt_steps)
            def _():
                outs[1][rows, :] = y
        else:
            outs[0][rows, :] = y
            outs[1][rows, :] = y.astype(BF16)

    @pl.when(i == 0)
    def _():
        gather(slot_ref, 0, 0)

    drain(0)
    gather(slot_ref, tm, 1)
    reduce_tile(0, 0)
    drain(1)

    @pl.when(i + 1 < n_steps)
    def _():
        gather(slot_next_ref, 0, 0)

    reduce_tile(1, 1)


def _combine(slot_t, gates, x1, shared, ys, g, b, t_prompt, final):
    t, d = x1.shape
    tm = _row_tile(math.gcd(t_prompt, t - t_prompt) // 2, 128)
    step = 2 * tm
    nps = t_prompt // step
    n_steps = t // step
    assert nps * step == t_prompt and n_steps * step == t
    vec = pl.BlockSpec((1, d), lambda i: (0, 0))
    rows = lambda w: pl.BlockSpec((step, w), lambda i: (i, 0))
    if final:
        out_specs = [pl.BlockSpec((step, d), lambda i: (jnp.minimum(i, nps - 1), 0)),
                     pl.BlockSpec((step, d), lambda i: (jnp.maximum(i - nps, 0), 0))]
        out_shape = [jax.ShapeDtypeStruct((t_prompt, d), F32), jax.ShapeDtypeStruct((t - t_prompt, d), F32)]
    else:
        out_specs = [rows(d), rows(d)]
        out_shape = [jax.ShapeDtypeStruct((t, d), F32), jax.ShapeDtypeStruct((t, d), BF16)]
    slots = _slot_tiles(slot_t, step)
    slot_spec = lambda shift: pl.BlockSpec((None, 1, step * TOP_K),
                                           lambda i: (jnp.minimum(i + shift, n_steps - 1), 0, 0),
                                           memory_space=pltpu.SMEM)
    return pl.pallas_call(
        functools.partial(_combine_kernel, n_prompt_steps=nps, n_steps=n_steps, final=final),
        grid=(n_steps,),
        in_specs=[slot_spec(0), slot_spec(1), rows(TOP_K), rows(d), rows(d), pl.BlockSpec(memory_space=pl.ANY),
                  vec, vec],
        out_specs=out_specs,
        out_shape=out_shape,
        scratch_shapes=[pltpu.VMEM((tm // SUBLANES, SUBLANES, d), F32)] * (2 * TOP_K) + [pltpu.SemaphoreType.DMA((2,))],
        compiler_params=_cparams(("arbitrary",), VMEM_LIMIT),
        name="combine",
    )(slots, slots, gates, x1, shared, ys, g, b)


def _filter_features(seq):
    t = np.linspace(0.0, 1.0, seq)[:, None]
    w = 2.0 * np.pi * np.arange(seq) / seq
    bands = np.linspace(1e-4, FILTER_BANDS - 1, FILTER_BANDS)
    ang = w[:, None] * bands[None, :]
    z = np.zeros((seq, FILTER_ORDER))
    z[:, :FILTER_EMB] = np.concatenate([t, np.cos(ang), -np.sin(ang)], axis=-1)
    return jnp.asarray(z, F32)


def _decay_rates():
    max_decay = math.log(DECAY_TARGET) / SHORT_DECAY_PCT
    min_decay = math.log(DECAY_TARGET) / LONG_DECAY_PCT
    return jnp.asarray(np.linspace(min_decay, max_decay, HYENA_WIDTH)[None, :], F32)


def kernel(x_prompt, x_sample, ln_in_g, ln_in_b, w_in, attn_sink, hy_conv_w, hy_conv_b, hy_f_w1, hy_f_b1,
           hy_f_freq, hy_f_w2, hy_f_b2, hy_f_w3, hy_skip, pool_w, pool_scale, out_norm_g, w_out, ln1_g, ln1_b,
           router_w, router_bias, exp_w_gate, exp_w_up, exp_w_down, sh_w_gate, sh_w_up, sh_w_down, ln2_g, ln2_b):
    bp, seq, d = x_prompt.shape
    bs = x_sample.shape[0]
    assert x_sample.shape[1] == seq and d == D_MODEL and seq % (FFT_N1 * SUBLANES) == 0
    nbatch = bp + bs
    t_prompt = bp * seq
    t = nbatch * seq
    n_blocks = t * TOP_K // MOE_MB + N_EXPERTS
    assert (t * TOP_K) % MOE_MB == 0

    plan = _FftPlan(seq)
    ctab, stab = _rope_tables(seq)
    w1p = jnp.pad(hy_f_w1, ((0, 0), (0, FILTER_ORDER - FILTER_EMB), (0, 0)))
    hspec = _hyena_filter_spectrum(plan, _filter_features(seq), w1p, hy_f_b1[:, None, :], hy_f_freq[:, None, :],
                                   hy_f_w2, hy_f_b2[:, None, :], hy_f_w3, _decay_rates())

    row = lambda v: v.reshape(1, -1)
    x, xb = _ln_in(x_prompt.reshape(t_prompt, d), x_sample.reshape(bs * seq, d), row(ln_in_g), row(ln_in_b))
    outs = None
    for l in range(DEPTH):
        proj3 = _in_proj(xb, w_in[l].astype(BF16)).reshape(nbatch, seq, IN_WIDTH)
        ya = _attention(proj3, attn_sink[l], ctab, stab)
        hsrc, hgate = _hyena_pre(proj3, hy_conv_w[l], row(hy_conv_b[l]))
        yh = _hyena(plan, hsrc, hgate, row(hy_skip[l]), hspec, l)
        yp = _pool(proj3, pool_w[l].astype(BF16), row(pool_scale[l]))
        x1, xpk, eidx_t, gate_t, rank_t, counts = _post_mixer(
            ya.reshape(t, ATTN_WIDTH), yh.reshape(t, HYENA_WIDTH), yp.reshape(t, POOL_WIDTH), x,
            row(out_norm_g[l]), w_out[l].astype(BF16), row(ln1_g[l]), row(ln1_b[l]), router_w[l], router_bias[l])
        slot_t, block_expert, n_used, last_block, has_rows = _dispatch_plan(eidx_t, rank_t, counts, n_blocks)
        xs, shared = _dispatch(xpk, slot_t, n_used, last_block, has_rows, sh_w_gate[l].astype(BF16),
                               sh_w_up[l].astype(BF16), sh_w_down[l].astype(BF16), n_blocks)
        ys = _experts(xs, block_expert, n_used, exp_w_gate, exp_w_up, exp_w_down, l, n_blocks)
        res = _combine(slot_t, gate_t.T, x1, shared, ys, row(ln2_g[l]), row(ln2_b[l]), t_prompt,
                       final=(l == DEPTH - 1))
        if l == DEPTH - 1:
            outs = res
        else:
            x, xb = res
    return outs[0].reshape(bp, seq, d), outs[1].reshape(bs, seq, d)
```

```python
import functools
import math

import jax
import jax.numpy as jnp
import numpy as np
from jax import lax
from jax.experimental import pallas as pl
from jax.experimental.pallas import tpu as pltpu

F32 = jnp.float32
BF16 = jnp.bfloat16
HIGHEST = lax.Precision.HIGHEST

D_MODEL = 2048
DEPTH = 2
HEAD_DIM = 128
ATTN_WIDTH = D_MODEL // 2
N_Q_HEADS = ATTN_WIDTH // HEAD_DIM
N_KV_HEADS = 2
Q_PER_KV = N_Q_HEADS // N_KV_HEADS
KV_WIDTH = N_KV_HEADS * HEAD_DIM
HYENA_WIDTH = D_MODEL // 4
POOL_WIDTH = D_MODEL - ATTN_WIDTH - HYENA_WIDTH
IN_WIDTH = ATTN_WIDTH + 2 * KV_WIDTH + 3 * HYENA_WIDTH + POOL_WIDTH
OFF_K = ATTN_WIDTH
OFF_V = OFF_K + KV_WIDTH
OFF_HY = OFF_V + KV_WIDTH
OFF_POOL = OFF_HY + 3 * HYENA_WIDTH

WINDOW = 128
BLOCK = 128
ROPE_THETA = 500000.0
ROPE_DIM = HEAD_DIM // 4
ROPE_HALF = ROPE_DIM // 2
NEG_BIG = -1e30
REMOVED = -3.0e38

SHORT_CONV = 3
FILTER_EMB = 33
FILTER_BANDS = (FILTER_EMB - 1) // 2
FILTER_ORDER = 64
DECAY_TARGET = 1e-2
SHORT_DECAY_PCT = 0.3
LONG_DECAY_PCT = 1.5

POOL_WINDOWS = (2, 4, 8, 16)
POOL_GROUP = POOL_WIDTH // len(POOL_WINDOWS)
OUT_GROUP = 128

N_EXPERTS = 64
TOP_K = 8
N_EXPERT_GROUPS = 8
EXPERTS_PER_GROUP = N_EXPERTS // N_EXPERT_GROUPS
TOPK_GROUPS = 4
EXPERT_HIDDEN = 512
SHARED_HIDDEN = 512
ROUTED_SCALE = 2.5

ALPHA = (2 * DEPTH) ** 0.25
LN_EPS = 1e-5
RMS_EPS = 1e-6

SUBLANES = 8
LANES = 128
VMEM_LIMIT = 56 * 1024 * 1024

FFT_N1 = 64
HY_CB = 128
MOE_MB = 512
HALF_D = D_MODEL // 2


def _cparams(sem, vmem=None):
    return pltpu.CompilerParams(dimension_semantics=sem, vmem_limit_bytes=vmem)


def _row_tile(t, pref):
    while t % pref:
        pref //= 2
    return pref


def _ln_rows(x, g, b):
    mu = jnp.mean(x, axis=-1, keepdims=True)
    xc = x - mu
    var = jnp.mean(xc * xc, axis=-1, keepdims=True)
    return xc * lax.rsqrt(var + LN_EPS) * g + b


def _pack_bf16_pair(x):
    h = x.shape[1] // 2
    hi = lax.bitcast_convert_type(x[:, :h].astype(BF16).astype(F32), jnp.uint32)
    lo = lax.bitcast_convert_type(x[:, h:].astype(BF16).astype(F32), jnp.uint32)
    return hi | (lo >> 16)


def _unpack_bf16_pair(pk):
    a = lax.bitcast_convert_type(pk & jnp.uint32(0xFFFF0000), F32).astype(BF16)
    b = lax.bitcast_convert_type(pk << 16, F32).astype(BF16)
    return a, b


def _ln_in_kernel(xp_ref, xs_ref, g_ref, b_ref, o_ref, ob_ref, *, n_prompt_blocks):
    i = pl.program_id(0)

    def emit(x):
        y = _ln_rows(x, g_ref[...], b_ref[...])
        o_ref[...] = y
        ob_ref[...] = y.astype(BF16)

    @pl.when(i < n_prompt_blocks)
    def _():
        emit(xp_ref[...])

    @pl.when(i >= n_prompt_blocks)
    def _():
        emit(xs_ref[...])


def _ln_in(xp, xs, g, b):
    tp, d = xp.shape
    ts = xs.shape[0]
    tm = _row_tile(math.gcd(tp, ts), 512)
    npb, nsb = tp // tm, ts // tm
    t = tp + ts
    return pl.pallas_call(
        functools.partial(_ln_in_kernel, n_prompt_blocks=npb),
        grid=(npb + nsb,),
        in_specs=[
            pl.BlockSpec((tm, d), lambda i: (jnp.minimum(i, npb - 1), 0)),
            pl.BlockSpec((tm, d), lambda i: (jnp.maximum(i - npb, 0), 0)),
            pl.BlockSpec((1, d), lambda i: (0, 0)),
            pl.BlockSpec((1, d), lambda i: (0, 0)),
        ],
        out_specs=[pl.BlockSpec((tm, d), lambda i: (i, 0)), pl.BlockSpec((tm, d), lambda i: (i, 0))],
        out_shape=[jax.ShapeDtypeStruct((t, d), F32), jax.ShapeDtypeStruct((t, d), BF16)],
        compiler_params=_cparams(("arbitrary",)),
        name="ln_in",
    )(xp, xs, g, b)


def _matmul_kernel(x_ref, w_ref, o_ref):
    o_ref[...] = jnp.dot(x_ref[...], w_ref[...], preferred_element_type=F32).astype(o_ref.dtype)


def _in_proj(xb, w):
    t, k = xb.shape
    n = w.shape[1]
    tm = _row_tile(t, 1024)
    tn = 512
    return pl.pallas_call(
        _matmul_kernel,
        grid=(t // tm, n // tn),
        in_specs=[pl.BlockSpec((tm, k), lambda i, j: (i, 0)), pl.BlockSpec((k, tn), lambda i, j: (0, j))],
        out_specs=pl.BlockSpec((tm, tn), lambda i, j: (i, j)),
        out_shape=jax.ShapeDtypeStruct((t, n), BF16),
        compiler_params=_cparams(("arbitrary", "arbitrary")),
        name="in_proj",
    )(xb, w)


def _rope_tables(seq):
    pos = np.arange(-BLOCK, seq + BLOCK, dtype=np.float64)
    inv = ROPE_THETA ** (-np.arange(ROPE_HALF, dtype=np.float64) / ROPE_HALF)
    ang = pos[:, None] * inv[None, :]
    c = np.ones((pos.shape[0], HEAD_DIM))
    s = np.zeros((pos.shape[0], HEAD_DIM))
    c[:, :ROPE_HALF] = np.cos(ang)
    c[:, ROPE_HALF:ROPE_DIM] = np.cos(ang)
    s[:, :ROPE_HALF] = -np.sin(ang)
    s[:, ROPE_HALF:ROPE_DIM] = np.sin(ang)
    return jnp.asarray(c, F32), jnp.asarray(s, F32)


def _rope(x, c, s):
    lane = lax.broadcasted_iota(jnp.int32, x.shape, 1)
    partner = jnp.where(lane < ROPE_HALF, pltpu.roll(x, HEAD_DIM - ROPE_HALF, 1), pltpu.roll(x, ROPE_HALF, 1))
    return x * c + partner * s


def _attn_kernel(sink_ref, q_ref, *refs, nb, qb):
    nkb = qb + 2
    k_refs, v_refs = refs[:nkb], refs[nkb:2 * nkb]
    ct_ref, st_ref, o_ref = refs[2 * nkb:]
    i = pl.program_id(1)
    base = pl.multiple_of(i * (qb * BLOCK), BLOCK)
    cw = ct_ref[pl.ds(base, nkb * BLOCK), :]
    sw = st_ref[pl.ds(base, nkb * BLOCK), :]
    kwin = jnp.concatenate([r[0] for r in k_refs], axis=0).astype(F32)
    vwin = jnp.concatenate([r[0] for r in v_refs], axis=0)
    q = q_ref[0].astype(F32)

    rows = Q_PER_KV * BLOCK
    rr = lax.broadcasted_iota(jnp.int32, (rows, 3 * BLOCK), 0) & (BLOCK - 1)
    cc = lax.broadcasted_iota(jnp.int32, (rows, 3 * BLOCK), 1)
    rel = cc - rr
    in_window = (rel >= BLOCK - WINDOW) & (rel <= BLOCK + WINDOW)
    rgrp = lax.broadcasted_iota(jnp.int32, (rows, 1), 0) // BLOCK

    for h in range(N_KV_HEADS):
        kh_all = _rope(kwin[:, h * HEAD_DIM:(h + 1) * HEAD_DIM], cw, sw).astype(BF16)
        vh_all = vwin[:, h * HEAD_DIM:(h + 1) * HEAD_DIM].astype(BF16)
        sink = jnp.zeros((rows, 1), F32)
        for g in range(Q_PER_KV):
            sink = jnp.where(rgrp == g, sink_ref[h * Q_PER_KV + g], sink)
        for sb in range(qb):
            blk = i * qb + sb
            lo = jnp.where(blk == 0, BLOCK, 0)
            hi = jnp.where(blk == nb - 1, 2 * BLOCK, 3 * BLOCK)
            valid = in_window & (cc >= lo) & (cc < hi)
            cq, sq = cw[(sb + 1) * BLOCK:(sb + 2) * BLOCK], sw[(sb + 1) * BLOCK:(sb + 2) * BLOCK]
            kh, vh = kh_all[sb * BLOCK:(sb + 3) * BLOCK], vh_all[sb * BLOCK:(sb + 3) * BLOCK]
            qs = []
            for g in range(Q_PER_KV):
                hq = h * Q_PER_KV + g
                qs.append(_rope(q[sb * BLOCK:(sb + 1) * BLOCK, hq * HEAD_DIM:(hq + 1) * HEAD_DIM], cq, sq).astype(BF16))
            qg = jnp.concatenate(qs, axis=0)
            s = lax.dot_general(qg, kh, (((1,), (1,)), ((), ())), preferred_element_type=F32) * (HEAD_DIM ** -0.5)
            s = jnp.where(valid, s, NEG_BIG)
            m = jnp.maximum(jnp.max(s, axis=-1, keepdims=True), sink)
            p = jnp.exp(s - m)
            denom = jnp.sum(p, axis=-1, keepdims=True) + jnp.exp(sink - m)
            o = jnp.dot((p / denom).astype(BF16), vh, preferred_element_type=F32)
            for g in range(Q_PER_KV):
                hq = h * Q_PER_KV + g
                o_ref[0, sb * BLOCK:(sb + 1) * BLOCK, hq * HEAD_DIM:(hq + 1) * HEAD_DIM] = o[g * BLOCK:(g + 1) * BLOCK]


def _attention(proj3, sink, ctab, stab):
    b, seq, _ = proj3.shape
    nb = seq // BLOCK
    qb = next(q for q in (4, 2, 1) if nb % q == 0)
    kcol, vcol = OFF_K // KV_WIDTH, OFF_V // KV_WIDTH

    def kv_spec(col, shift):
        return pl.BlockSpec((1, BLOCK, KV_WIDTH), lambda bi, i: (bi, jnp.clip(i * qb + shift, 0, nb - 1), col))

    shifts = range(-1, qb + 1)
    n_kv = 2 * len(shifts)
    return pl.pallas_call(
        functools.partial(_attn_kernel, nb=nb, qb=qb),
        grid=(b, nb // qb),
        in_specs=[pl.BlockSpec(memory_space=pltpu.SMEM),
                  pl.BlockSpec((1, qb * BLOCK, ATTN_WIDTH), lambda bi, i: (bi, i, 0))]
                 + [kv_spec(kcol, s) for s in shifts] + [kv_spec(vcol, s) for s in shifts]
                 + [pl.BlockSpec(ctab.shape, lambda bi, i: (0, 0)), pl.BlockSpec(stab.shape, lambda bi, i: (0, 0))],
        out_specs=pl.BlockSpec((1, qb * BLOCK, ATTN_WIDTH), lambda bi, i: (bi, i, 0)),
        out_shape=jax.ShapeDtypeStruct((b, seq, ATTN_WIDTH), F32),
        compiler_params=_cparams(("arbitrary", "arbitrary")),
        name="attention",
    )(sink, *([proj3] * (1 + n_kv)), ctab, stab)


class _FftPlan:
    def __init__(self, seq):
        self.seq = seq
        self.n = 2 * seq
        self.n1 = FFT_N1
        self.n2 = self.n // self.n1
        self.nh = seq // self.n1
        self.k2 = self.n2 // 2 + 1
        self.k2p = -(-self.k2 // SUBLANES) * SUBLANES
        self.srow = 2 * self.n1 + SUBLANES
        n1 = np.arange(self.n1)
        n2 = np.arange(self.nh)
        k2 = np.arange(self.k2)
        npos = self.n1 * n2[None, None, :] + n1[:, None, None]
        ang = 2.0 * np.pi * k2[None, :, None] * npos / self.n
        f1 = np.zeros((self.n1, 2 * self.k2p, self.nh))
        f1[:, :self.k2] = np.cos(ang)
        f1[:, self.k2p:self.k2p + self.k2] = -np.sin(ang)
        wk = np.full((self.k2,), 2.0)
        wk[0] = 1.0
        wk[-1] = 1.0
        gm = np.zeros((self.n1, self.nh, 2 * self.k2p))
        angt = np.transpose(ang, (0, 2, 1))
        gm[:, :, :self.k2] = np.cos(angt) * wk / self.n
        gm[:, :, self.k2p:self.k2p + self.k2] = -np.sin(angt) * wk / self.n
        a2 = 2.0 * np.pi * np.outer(n1, n1) / self.n1
        c2, s2 = np.cos(a2), np.sin(a2)
        self.f1 = _split3_lhs(f1)
        self.g = _split3_lhs(gm)
        self.m2 = _split3_lhs(np.block([[c2, s2], [-s2, c2]]))
        self.m2i = _split3_lhs(np.block([[c2, -s2], [s2, c2]]))


def _dotf(a, b):
    return jnp.dot(a, b, precision=HIGHEST, preferred_element_type=F32)


def _split3_lhs(m):
    m32 = np.asarray(m, np.float32)
    hi = m32.astype(BF16)
    lo = (m32 - hi.astype(np.float32)).astype(BF16)
    return jnp.asarray(np.concatenate([hi, hi, lo], axis=-1))


def _split3_rhs(x):
    hi = x.astype(BF16)
    lo = (x - hi.astype(F32)).astype(BF16)
    return jnp.concatenate([hi, lo, hi], axis=0)


def _dot3(lhs3, x):
    return jnp.dot(lhs3, _split3_rhs(x), preferred_element_type=F32)


def _lane_cat(parts):
    return parts[0] if len(parts) == 1 else jnp.concatenate(parts, axis=1)


def _fft_stage1(plan, load_rows, f1_ref, slab_refs):
    unroll = 4

    def body(grp, carry):
        for u in range(unroll):
            n1 = grp * unroll + u
            res = _dot3(f1_ref[n1], _lane_cat([ld(n1) for ld in load_rows]))
            for p, slab in enumerate(slab_refs):
                part = res[:, p * LANES:(p + 1) * LANES]
                slab[pl.ds(n1, plan.k2p, stride=plan.srow), :] = part[:plan.k2p]
                slab[pl.ds(plan.n1 + n1, plan.k2p, stride=plan.srow), :] = part[plan.k2p:]
        return carry

    lax.fori_loop(0, plan.n1 // unroll, body, 0)


def _slab_rows(plan, slab_refs, k2):
    off = pl.multiple_of(k2 * plan.srow, SUBLANES)
    return off, _lane_cat([s[pl.ds(off, 2 * plan.n1), :] for s in slab_refs])


def _hyena_filter_kernel(z_ref, w1_ref, b1_ref, fr_ref, w2_ref, b2_ref, w3f_ref, w3b_ref, dl_ref,
                         f1_ref, m2_ref, h_ref, srcf_ref, srcb_ref, slabf_ref, slabb_ref, hid_ref, *, plan):
    seq = plan.seq

    @pl.when(pl.program_id(1) == 0)
    def _():
        fr = fr_ref[0]
        h1 = jnp.sin(fr * (_dotf(z_ref[...], w1_ref[0]) + b1_ref[0]))
        hid_ref[...] = jnp.sin(fr * (_dotf(h1, w2_ref[0]) + b2_ref[0]))

    hid = hid_ref[...]
    row = lax.broadcasted_iota(jnp.int32, (seq, 1), 0)
    t = row.astype(F32) * (1.0 / (seq - 1))
    decay = jnp.exp(-t * jnp.abs(dl_ref[...]))
    srcf_ref[...] = _dotf(hid, w3f_ref[0]) * decay
    srcb_ref[...] = jnp.where(row == 0, 0.0, _dotf(hid, w3b_ref[0]) * decay)
    slabs = (slabf_ref, slabb_ref)
    _fft_stage1(plan, [lambda n1, r=r: r[pl.ds(n1, plan.nh, stride=plan.n1), :] for r in (srcf_ref, srcb_ref)],
                f1_ref, slabs)

    unroll = 5 if plan.k2 % 5 == 0 else 1

    def body(grp, carry):
        for u in range(unroll):
            k2 = grp * unroll + u
            _, a = _slab_rows(plan, slabs, k2)
            x = _dot3(m2_ref[...], a)
            h_ref[0, k2, :plan.n1, :] = x[:plan.n1, :LANES] + x[:plan.n1, LANES:]
            h_ref[0, k2, plan.n1:, :] = x[plan.n1:, :LANES] - x[plan.n1:, LANES:]
        return carry

    lax.fori_loop(0, plan.k2 // unroll, body, 0)


def _hyena_filter_spectrum(plan, z, w1p, b1, freq, w2, b2, w3, deltas):
    depth = w3.shape[0]
    cb = HY_CB
    ncb = HYENA_WIDTH // cb
    seq = plan.seq
    lay = lambda shp: pl.BlockSpec((1,) + shp, lambda l, j: (l,) + (0,) * len(shp))
    return pl.pallas_call(
        functools.partial(_hyena_filter_kernel, plan=plan),
        grid=(depth, ncb),
        in_specs=[
            pl.BlockSpec(z.shape, lambda l, j: (0, 0)),
            lay(w1p.shape[1:]), lay(b1.shape[1:]), lay(freq.shape[1:]), lay(w2.shape[1:]), lay(b2.shape[1:]),
            pl.BlockSpec((1, FILTER_ORDER, cb), lambda l, j: (l, 0, j)),
            pl.BlockSpec((1, FILTER_ORDER, cb), lambda l, j: (l, 0, ncb + j)),
            pl.BlockSpec((1, cb), lambda l, j: (0, j)),
            pl.BlockSpec(plan.f1.shape, lambda l, j: (0, 0, 0)),
            pl.BlockSpec(plan.m2.shape, lambda l, j: (0, 0)),
        ],
        out_specs=pl.BlockSpec((1, plan.k2, 2 * plan.n1, cb), lambda l, j: (l, 0, 0, j)),
        out_shape=jax.ShapeDtypeStruct((depth, plan.k2, 2 * plan.n1, HYENA_WIDTH), F32),
        scratch_shapes=[pltpu.VMEM((seq, cb), F32), pltpu.VMEM((seq, cb), F32),
                        pltpu.VMEM((plan.k2p * plan.srow, cb), F32), pltpu.VMEM((plan.k2p * plan.srow, cb), F32),
                        pltpu.VMEM((seq, FILTER_ORDER), F32)],
        compiler_params=_cparams(("arbitrary", "arbitrary"), VMEM_LIMIT),
        name="hyena_filter",
    )(z, w1p, b1, freq, w2, b2, w3, w3, deltas, plan.f1, plan.m2)


def _short_conv(u, w, b, seq):
    row = lax.broadcasted_iota(jnp.int32, (seq, 1), 0)
    prev = jnp.where(row >= 1, pltpu.roll(u, 1, 0), 0.0)
    nxt = jnp.where(row < seq - 1, pltpu.roll(u, seq - 1, 0), 0.0)
    return prev * w[0:1] + u * w[1:2] + nxt * w[2:3] + b


def _hyena_pre_kernel(x0_ref, x1_ref, v_ref, w0_ref, w1_ref, wv_ref, b0_ref, b1_ref, bv_ref, src_ref, gate_ref, *, seq):
    gate_ref[0] = _short_conv(x0_ref[0].astype(F32), w0_ref[...], b0_ref[...], seq)
    src_ref[0] = (_short_conv(v_ref[0].astype(F32), wv_ref[...], bv_ref[...], seq)
                  * _short_conv(x1_ref[0].astype(F32), w1_ref[...], b1_ref[...], seq))


def _hyena_pre(proj3, conv_w, conv_b):
    b, seq, _ = proj3.shape
    cb = HY_CB
    ncb = HYENA_WIDTH // cb
    c0 = OFF_HY // cb

    def u_spec(part):
        return pl.BlockSpec((1, seq, cb), lambda bi, j: (bi, 0, c0 + part * ncb + j))

    def w_spec(part):
        return pl.BlockSpec((SHORT_CONV, cb), lambda bi, j: (0, part * ncb + j))

    def b_spec(part):
        return pl.BlockSpec((1, cb), lambda bi, j: (0, part * ncb + j))

    out = pl.BlockSpec((1, seq, cb), lambda bi, j: (bi, 0, j))
    shp = jax.ShapeDtypeStruct((b, seq, HYENA_WIDTH), F32)
    return pl.pallas_call(
        functools.partial(_hyena_pre_kernel, seq=seq),
        grid=(b, ncb),
        in_specs=[u_spec(0), u_spec(1), u_spec(2), w_spec(0), w_spec(1), w_spec(2), b_spec(0), b_spec(1), b_spec(2)],
        out_specs=[out, out],
        out_shape=[shp, shp],
        compiler_params=_cparams(("arbitrary", "arbitrary"), VMEM_LIMIT),
        name="hyena_pre",
    )(proj3, proj3, proj3, conv_w, conv_w, conv_w, conv_b, conv_b, conv_b)


def _hyena_kernel(src_ref, gate_ref, skip_ref, h_ref, f1_ref, m2_ref, m2i_ref, g_ref, o_ref, *slab_refs, plan):
    npar = len(slab_refs)
    n1c = plan.n1
    _fft_stage1(plan, [lambda n1, p=p: src_ref[p, pl.ds(n1, plan.nh, stride=n1c), :] for p in range(npar)],
                f1_ref, slab_refs)

    unroll2 = 5 if plan.k2 % 5 == 0 else 1

    def stage2(grp, carry):
        for u in range(unroll2):
            k2 = grp * unroll2 + u
            off, a = _slab_rows(plan, slab_refs, k2)
            x = _dot3(m2_ref[...], a)
            h = _lane_cat([h_ref[0, k2]] * npar)
            xr, xi = x[:n1c], x[n1c:]
            hr, hi = h[:n1c], h[n1c:]
            y = jnp.concatenate([xr * hr - xi * hi, xr * hi + xi * hr], axis=0)
            back = _dot3(m2i_ref[...], y)
            for p, slab in enumerate(slab_refs):
                slab[pl.ds(off, 2 * n1c), :] = back[:, p * LANES:(p + 1) * LANES]
        return carry

    lax.fori_loop(0, plan.k2 // unroll2, stage2, 0)

    unroll3 = 4

    def stage3(grp, carry):
        for u in range(unroll3):
            n1 = grp * unroll3 + u
            a = _lane_cat([jnp.concatenate([s[pl.ds(n1, plan.k2p, stride=plan.srow), :],
                                            s[pl.ds(n1c + n1, plan.k2p, stride=plan.srow), :]], axis=0)
                           for s in slab_refs])
            y = _dot3(g_ref[n1], a)
            for p in range(npar):
                o_ref[p, pl.ds(n1, plan.nh, stride=n1c), :] = y[:, p * LANES:(p + 1) * LANES]
        return carry

    lax.fori_loop(0, n1c // unroll3, stage3, 0)
    for p in range(npar):
        o_ref[p] = (o_ref[p] + src_ref[p] * skip_ref[...]) * gate_ref[p]


def _hyena(plan, src, gate, skip, hspec, layer):
    b, seq, _ = src.shape
    cb = HY_CB
    ncb = HYENA_WIDTH // cb
    npar = 2 if b % 2 == 0 else 1
    once = pl.Buffered(1)
    const = lambda a: pl.BlockSpec(a.shape, lambda j, bi: (0,) * a.ndim, pipeline_mode=once)
    seq_spec = pl.BlockSpec((npar, seq, cb), lambda j, bi: (bi, 0, j))
    return pl.pallas_call(
        functools.partial(_hyena_kernel, plan=plan),
        grid=(ncb, b // npar),
        in_specs=[
            seq_spec, seq_spec,
            pl.BlockSpec((1, cb), lambda j, bi: (0, j)),
            pl.BlockSpec((1, plan.k2, 2 * plan.n1, cb), lambda j, bi: (layer, 0, 0, j), pipeline_mode=once),
            const(plan.f1), const(plan.m2), const(plan.m2i), const(plan.g),
        ],
        out_specs=seq_spec,
        out_shape=jax.ShapeDtypeStruct((b, seq, HYENA_WIDTH), F32),
        scratch_shapes=[pltpu.VMEM((plan.k2p * plan.srow, cb), F32)] * npar,
        compiler_params=_cparams(("arbitrary", "arbitrary"), VMEM_LIMIT),
        name="hyena",
    )(src, gate, skip, hspec, plan.f1, plan.m2, plan.m2i, plan.g)


def _pool_kernel(u_ref, w_ref, sc_ref, o_ref, *, seq):
    row = lax.broadcasted_iota(jnp.int32, (seq, 1), 0)

    def back(x, s):
        return jnp.where(row >= s, pltpu.roll(x, s, 0), 0.0)

    def fwd(x, s):
        return jnp.where(row < seq - s, pltpu.roll(x, seq - s, 0), 0.0)

    for gi, win in enumerate(POOL_WINDOWS):
        half = win // 2
        u = u_ref[0, :, gi * POOL_GROUP:(gi + 1) * POOL_GROUP].astype(F32)
        ahead, behind, s = u, u, 1
        while s < half:
            ahead = ahead + fwd(ahead, s)
            behind = behind + back(behind, s)
            s *= 2
        total = ahead + back(behind, 1)
        cnt = (jnp.minimum(row + half, seq) - jnp.maximum(row - half, 0)).astype(F32)
        diff = total / cnt - u
        y = jnp.dot(diff.astype(BF16), w_ref[gi], preferred_element_type=F32)
        o_ref[0, :, gi * POOL_GROUP:(gi + 1) * POOL_GROUP] = y * sc_ref[:, gi * POOL_GROUP:(gi + 1) * POOL_GROUP]


def _pool(proj3, pool_w, pool_scale):
    b, seq, _ = proj3.shape
    return pl.pallas_call(
        functools.partial(_pool_kernel, seq=seq),
        grid=(b,),
        in_specs=[
            pl.BlockSpec((1, seq, POOL_WIDTH), lambda bi: (bi, 0, OFF_POOL // POOL_WIDTH)),
            pl.BlockSpec(pool_w.shape, lambda bi: (0, 0, 0)),
            pl.BlockSpec((1, POOL_WIDTH), lambda bi: (0, 0)),
        ],
        out_specs=pl.BlockSpec((1, seq, POOL_WIDTH), lambda bi: (bi, 0, 0)),
        out_shape=jax.ShapeDtypeStruct((b, seq, POOL_WIDTH), F32),
        compiler_params=_cparams(("arbitrary",), VMEM_LIMIT),
        name="pool",
    )(proj3, pool_w, pool_scale)


def _post_mixer_kernel(ya_ref, yh_ref, yp_ref, x_ref, gn_ref, w_ref, g_ref, b_ref, rw_ref, bias_ref, tri_ref,
                       o_ref, opk_ref, eidx_ref, gate_ref, rank_ref, cnt_ref, carry_ref):
    parts = []
    for src in (ya_ref, yh_ref, yp_ref):
        for j in range(src.shape[1] // OUT_GROUP):
            c = src[:, j * OUT_GROUP:(j + 1) * OUT_GROUP]
            parts.append(c * lax.rsqrt(jnp.mean(c * c, axis=-1, keepdims=True) + RMS_EPS))
    yn = (jnp.concatenate(parts, axis=-1) * gn_ref[...]).astype(BF16)
    mix = jnp.dot(yn, w_ref[...], preferred_element_type=F32)
    x1 = _ln_rows(ALPHA * x_ref[...] + mix, g_ref[...], b_ref[...])
    o_ref[...] = x1
    opk_ref[...] = _pack_bf16_pair(x1)
    _route(x1, rw_ref, bias_ref, tri_ref, eidx_ref, gate_ref, rank_ref, cnt_ref, carry_ref)


def _post_mixer(ya, yh, yp, x, gn, w_out, g, b, router_w, router_bias):
    t, d = x.shape
    tm = _row_tile(t, 512)
    row = lambda w: pl.BlockSpec((tm, w), lambda i: (i, 0))
    vec = pl.BlockSpec((1, d), lambda i: (0, 0))
    once = pl.Buffered(1)
    const = lambda a: pl.BlockSpec(a.shape, lambda i: (0,) * a.ndim, pipeline_mode=once)
    rw3 = _split3_lhs_traced(router_w.T)
    bias_b = jnp.broadcast_to(router_bias[:, None], (N_EXPERTS, tm)).astype(F32)
    tri = jnp.asarray(np.triu(np.ones((tm, tm)), 1), BF16)
    kt = lambda dt: jax.ShapeDtypeStruct((TOP_K, t), dt)
    kspec = pl.BlockSpec((TOP_K, tm), lambda i: (0, i))
    return pl.pallas_call(
        _post_mixer_kernel,
        grid=(t // tm,),
        in_specs=[row(ATTN_WIDTH), row(HYENA_WIDTH), row(POOL_WIDTH), row(d), vec, const(w_out), vec, vec,
                  const(rw3), const(bias_b), const(tri)],
        out_specs=[row(d), row(d // 2), kspec, kspec, kspec, pl.BlockSpec((N_EXPERTS, LANES), lambda i: (0, 0))],
        out_shape=[jax.ShapeDtypeStruct((t, d), F32), jax.ShapeDtypeStruct((t, d // 2), jnp.uint32),
                   kt(jnp.int32), kt(F32), kt(jnp.int32), jax.ShapeDtypeStruct((N_EXPERTS, LANES), F32)],
        scratch_shapes=[pltpu.VMEM((N_EXPERTS, LANES), F32)],
        compiler_params=_cparams(("arbitrary",), VMEM_LIMIT),
        name="post_mixer",
    )(ya, yh, yp, x, gn, w_out, g, b, rw3, bias_b, tri)


def _split3_lhs_traced(m):
    hi = m.astype(BF16)
    lo = (m - hi.astype(F32)).astype(BF16)
    return jnp.concatenate([hi, hi, lo], axis=-1)


def _route(x1, rw_ref, bias_ref, tri_ref, eidx_ref, gate_ref, rank_ref, cnt_ref, carry_ref):
    tm = x1.shape[0]

    @pl.when(pl.program_id(0) == 0)
    def _():
        carry_ref[...] = jnp.zeros_like(carry_ref)

    xh = x1.astype(BF16)
    xl = (x1 - xh.astype(F32)).astype(BF16)
    logits = lax.dot_general(rw_ref[...], jnp.concatenate([xh, xl, xh], axis=1), (((1,), (1,)), ((), ())),
                             preferred_element_type=F32)
    scores = jax.nn.sigmoid(logits)
    choice = scores + bias_ref[...]
    sub = lax.broadcasted_iota(jnp.int32, (EXPERTS_PER_GROUP, tm), 0)
    far = jnp.int32(N_EXPERTS)

    def first_argmax(v):
        m = jnp.max(v, axis=0, keepdims=True)
        return m, jnp.min(jnp.where(v == m, sub, far), axis=0, keepdims=True)

    tiles, stiles, gscore = [], [], []
    for g in range(N_EXPERT_GROUPS):
        c = choice[g * EXPERTS_PER_GROUP:(g + 1) * EXPERTS_PER_GROUP]
        tiles.append(c)
        stiles.append(scores[g * EXPERTS_PER_GROUP:(g + 1) * EXPERTS_PER_GROUP])
        m1, i1 = first_argmax(c)
        m2 = jnp.max(jnp.where(sub == i1, REMOVED, c), axis=0, keepdims=True)
        gscore.append(m1 + m2)
    cur = jnp.concatenate(gscore, axis=0)
    gsel = jnp.zeros(cur.shape, F32)
    for _ in range(TOPK_GROUPS):
        _, ig = first_argmax(cur)
        hit = sub == ig
        gsel = jnp.where(hit, 1.0, gsel)
        cur = jnp.where(hit, REMOVED, cur)

    eids = [sub + g * EXPERTS_PER_GROUP for g in range(N_EXPERT_GROUPS)]
    masked = [jnp.where(gsel[g:g + 1] > 0.5, tiles[g], NEG_BIG) for g in range(N_EXPERT_GROUPS)]
    picked = [jnp.zeros((EXPERTS_PER_GROUP, tm), F32) for _ in range(N_EXPERT_GROUPS)]
    idxs, sels = [], []
    for _ in range(TOP_K):
        m = functools.reduce(jnp.maximum, [jnp.max(v, axis=0, keepdims=True) for v in masked])
        idx = functools.reduce(jnp.minimum, [jnp.min(jnp.where(v == m, e, far), axis=0, keepdims=True)
                                             for v, e in zip(masked, eids)])
        sc = jnp.zeros((1, tm), F32)
        for g in range(N_EXPERT_GROUPS):
            hit = eids[g] == idx
            sc = sc + jnp.sum(jnp.where(hit, stiles[g], 0.0), axis=0, keepdims=True)
            masked[g] = jnp.where(hit, REMOVED, masked[g])
            picked[g] = jnp.where(hit, 1.0, picked[g])
        idxs.append(idx)
        sels.append(sc)
    total = functools.reduce(lambda a, c: a + c, sels)
    for k in range(TOP_K):
        eidx_ref[k:k + 1, :] = idxs[k]
        gate_ref[k:k + 1, :] = sels[k] / total * ROUTED_SCALE

    sel = jnp.concatenate(picked, axis=0)
    before = jnp.dot(sel.astype(BF16), tri_ref[...], preferred_element_type=F32) + carry_ref[:, 0:1]
    for k in range(TOP_K):
        r = jnp.zeros((1, tm), F32)
        for g in range(N_EXPERT_GROUPS):
            r = r + jnp.sum(jnp.where(eids[g] == idxs[k], before[g * EXPERTS_PER_GROUP:(g + 1) * EXPERTS_PER_GROUP], 0.0),
                            axis=0, keepdims=True)
        rank_ref[k:k + 1, :] = r.astype(jnp.int32)
    carry_ref[...] = carry_ref[...] + jnp.sum(sel, axis=1, keepdims=True)
    cnt_ref[...] = carry_ref[...]


def _dispatch_plan(eidx_t, rank_t, counts, n_blocks):
    mb = MOE_MB
    cnt = counts[:, 0].astype(jnp.int32)
    padded = (cnt + mb - 1) // mb * mb
    pend = jnp.cumsum(padded)
    pstart = pend - padded
    experts = jnp.arange(N_EXPERTS, dtype=jnp.int32)
    slot_t = rank_t + jnp.sum(jnp.where(eidx_t[:, :, None] == experts, pstart, 0), axis=-1)
    block_end = pend // mb
    blocks = jnp.arange(n_blocks, dtype=jnp.int32)
    block_expert = jnp.minimum(jnp.sum((block_end[None, :] <= blocks[:, None]).astype(jnp.int32), axis=1),
                               N_EXPERTS - 1).astype(jnp.int32)
    n_used = block_end[-1:].astype(jnp.int32)
    last_block = (block_end - 1).astype(jnp.int32)
    has_rows = (padded > 0).astype(jnp.int32)
    return slot_t.astype(jnp.int32), block_expert, n_used, last_block, has_rows


def _dispatch_kernel(last_ref, has_ref, nused_ref, slot_ref, x_ref, wg_ref, wu_ref, wd_ref, xs_ref, sh_ref,
                     zero_ref, zsem, sem, *, n_blocks):
    mb = MOE_MB
    tm = x_ref.shape[0]

    def zero_copy(block):
        return pltpu.make_async_copy(zero_ref, xs_ref.at[pl.ds(pl.multiple_of(block * mb, mb), mb)], zsem)

    @pl.when(pl.program_id(0) == 0)
    def _():
        zero_ref[...] = jnp.zeros_like(zero_ref)

        def start_e(e, c):
            @pl.when(has_ref[e] > 0)
            def _():
                zero_copy(last_ref[e]).start()
            return c

        def wait_e(e, c):
            @pl.when(has_ref[e] > 0)
            def _():
                zero_copy(last_ref[e]).wait()
            return c

        def start_b(blk, c):
            zero_copy(blk).start()
            return c

        def wait_b(blk, c):
            zero_copy(blk).wait()
            return c

        lax.fori_loop(0, N_EXPERTS, start_e, 0)
        lax.fori_loop(nused_ref[0], n_blocks, start_b, 0)
        lax.fori_loop(0, N_EXPERTS, wait_e, 0)
        lax.fori_loop(nused_ref[0], n_blocks, wait_b, 0)

    groups = list(range(tm // SUBLANES))

    def scatter_rows(n_groups):
        for grp in groups[:n_groups]:
            for j in range(SUBLANES):
                r = grp * SUBLANES + j
                for k in range(TOP_K):
                    slot = slot_ref[0, r * TOP_K + k]
                    pltpu.make_async_copy(x_ref.at[pl.ds(r, 1)], xs_ref.at[pl.ds(slot, 1)], sem).start(priority=k % 2)
        del groups[:n_groups]

    chunk = 2 * LANES
    n_pieces = 2 * (SHARED_HIDDEN // chunk) + D_MODEL // chunk
    per_piece = -(-len(groups) // n_pieces)
    a, b = _unpack_bf16_pair(x_ref[...])
    pre = []
    for w_ref in (wg_ref, wu_ref):
        cols = []
        for c in range(SHARED_HIDDEN // chunk):
            scatter_rows(per_piece)
            cs = slice(c * chunk, (c + 1) * chunk)
            cols.append(jnp.dot(a, w_ref[:HALF_D, cs], preferred_element_type=F32)
                        + jnp.dot(b, w_ref[HALF_D:, cs], preferred_element_type=F32))
        pre.append(jnp.concatenate(cols, axis=1))
    hid = (pre[0] * jax.nn.sigmoid(pre[0]) * pre[1]).astype(BF16)
    for c in range(D_MODEL // chunk):
        scatter_rows(per_piece)
        cs = slice(c * chunk, (c + 1) * chunk)
        sh_ref[:, cs] = jnp.dot(hid, wd_ref[:, cs], preferred_element_type=F32)
    scatter_rows(len(groups))
    for k in range(TOP_K):
        pltpu.make_async_copy(x_ref, xs_ref.at[pl.ds(0, tm)], sem).wait()


def _slot_tiles(slot_t, tm):
    t = slot_t.shape[1]
    return slot_t.T.reshape(t // tm, 1, tm * TOP_K)


def _dispatch(xpk, slot_t, n_used, last_block, has_rows, wg, wu, wd, n_blocks):
    t, hw = xpk.shape
    tm = _row_tile(t, 256)
    full = lambda a: pl.BlockSpec(a.shape, lambda i, *_: (0,) * a.ndim)
    grid_spec = pltpu.PrefetchScalarGridSpec(
        num_scalar_prefetch=3,
        grid=(t // tm,),
        in_specs=[
            pl.BlockSpec((None, 1, tm * TOP_K), lambda i, *_: (i, 0, 0), memory_space=pltpu.SMEM),
            pl.BlockSpec((tm, hw), lambda i, *_: (i, 0)),
            full(wg), full(wu), full(wd),
        ],
        out_specs=[pl.BlockSpec(memory_space=pl.ANY), pl.BlockSpec((tm, D_MODEL), lambda i, *_: (i, 0))],
        scratch_shapes=[pltpu.VMEM((MOE_MB, hw), jnp.uint32), pltpu.SemaphoreType.DMA, pltpu.SemaphoreType.DMA],
    )
    return pl.pallas_call(
        functools.partial(_dispatch_kernel, n_blocks=n_blocks),
        grid_spec=grid_spec,
        out_shape=[jax.ShapeDtypeStruct((n_blocks * MOE_MB, hw), jnp.uint32), jax.ShapeDtypeStruct((t, D_MODEL), F32)],
        compiler_params=_cparams(("arbitrary",), VMEM_LIMIT),
        name="dispatch",
    )(last_block, has_rows, n_used, _slot_tiles(slot_t, tm), xpk, wg, wu, wd)


def _swiglu_packed(pk, wg, wu, wd):
    a, b = _unpack_bf16_pair(pk)
    gate = jnp.dot(a, wg[:HALF_D], preferred_element_type=F32) + jnp.dot(b, wg[HALF_D:], preferred_element_type=F32)
    up = jnp.dot(a, wu[:HALF_D], preferred_element_type=F32) + jnp.dot(b, wu[HALF_D:], preferred_element_type=F32)
    hid = (gate * jax.nn.sigmoid(gate) * up).astype(BF16)
    return jnp.dot(hid, wd, preferred_element_type=F32)


def _experts_kernel(be_ref, nused_ref, nxt_ref, xs_ref, wg_hbm, wu_hbm, wd_hbm, y_ref,
                    sg_ref, su_ref, sd_ref, wgb_ref, wub_ref, wdb_ref, sems, grp_ref, *, layer):
    i = pl.program_id(0)
    e = be_ref[i]
    first = (i == 0) | (e != be_ref[jnp.maximum(i - 1, 0)])

    def fetch(expert, slot):
        pairs = ((wg_hbm, sg_ref), (wu_hbm, su_ref), (wd_hbm, sd_ref))
        return [pltpu.make_async_copy(w.at[layer, expert], s.at[slot], sems.at[slot, j])
                for j, (w, s) in enumerate(pairs)]

    @pl.when(i == 0)
    def _():
        grp_ref[0] = 0
        for c in fetch(e, 0):
            c.start()

    @pl.when(first)
    def _():
        slot = grp_ref[0] & 1
        for c in fetch(e, slot):
            c.wait()
        nxt = nxt_ref[i]

        @pl.when(nxt != e)
        def _():
            for c in fetch(nxt, 1 - slot):
                c.start()

        wgb_ref[...] = sg_ref[slot].astype(BF16)
        wub_ref[...] = su_ref[slot].astype(BF16)
        wdb_ref[...] = sd_ref[slot].astype(BF16)
        grp_ref[0] = grp_ref[0] + 1

    @pl.when(i < nused_ref[0])
    def _():
        y_ref[...] = _swiglu_packed(xs_ref[...], wgb_ref, wub_ref, wdb_ref[...])

    @pl.when(i >= nused_ref[0])
    def _():
        y_ref[...] = jnp.zeros_like(y_ref)


def _next_expert(block_expert):
    experts = jnp.arange(N_EXPERTS, dtype=jnp.int32)
    present = jnp.any(block_expert[:, None] == experts[None, :], axis=0)
    later = jnp.where((experts[None, :] > block_expert[:, None]) & present[None, :], experts[None, :], N_EXPERTS)
    nxt = jnp.min(later, axis=1)
    return jnp.where(nxt == N_EXPERTS, block_expert, nxt).astype(jnp.int32)


def _experts(xs, block_expert, n_used, wg, wu, wd, layer, n_blocks):
    hw = xs.shape[1]
    d = wd.shape[3]
    hbm = pl.BlockSpec(memory_space=pl.ANY)
    grid_spec = pltpu.PrefetchScalarGridSpec(
        num_scalar_prefetch=3,
        grid=(n_blocks,),
        in_specs=[pl.BlockSpec((MOE_MB, hw), lambda i, *_: (i, 0)), hbm, hbm, hbm],
        out_specs=pl.BlockSpec((MOE_MB, d), lambda i, *_: (i, 0)),
        scratch_shapes=[pltpu.VMEM((2,) + wg.shape[2:], F32), pltpu.VMEM((2,) + wu.shape[2:], F32),
                        pltpu.VMEM((2,) + wd.shape[2:], F32),
                        pltpu.VMEM(wg.shape[2:], BF16), pltpu.VMEM(wu.shape[2:], BF16), pltpu.VMEM(wd.shape[2:], BF16),
                        pltpu.SemaphoreType.DMA((2, 3)), pltpu.SMEM((1,), jnp.int32)],
    )
    return pl.pallas_call(
        functools.partial(_experts_kernel, layer=layer),
        grid_spec=grid_spec,
        out_shape=jax.ShapeDtypeStruct((n_blocks * MOE_MB, d), F32),
        compiler_params=_cparams(("arbitrary",), VMEM_LIMIT),
        name="experts",
    )(block_expert, n_used, _next_expert(block_expert), xs, wg, wu, wd)


def _combine_kernel(slot_ref, slot_next_ref, gates_ref, x_ref, sh_ref, ys_ref, g_ref, b_ref, *rest,
                    n_prompt_steps, n_steps, final):
    outs = rest[:2]
    sets = (rest[2:2 + TOP_K], rest[2 + TOP_K:2 + 2 * TOP_K])
    sems = rest[2 + 2 * TOP_K]
    tm = x_ref.shape[0] // 2
    d = x_ref.shape[1]
    ngroups = tm // SUBLANES
    i = pl.program_id(0)

    def gather(table_ref, first_row, which):
        def body(grp, c):
            for j in range(SUBLANES):
                for k in range(TOP_K):
                    slot = table_ref[0, (first_row + grp * SUBLANES + j) * TOP_K + k]
                    pltpu.make_async_copy(ys_ref.at[pl.ds(slot, 1)], sets[which][k].at[grp, pl.ds(j, 1)],
                                          sems.at[which]).start(priority=k % 2)
            return c

        lax.fori_loop(0, ngroups, body, 0)

    def drain(which):
        def body(grp, c):
            for k in range(TOP_K):
                pltpu.make_async_copy(ys_ref.at[pl.ds(0, SUBLANES)], sets[which][k].at[grp], sems.at[which]).wait()
            return c

        lax.fori_loop(0, ngroups, body, 0)

    def reduce_tile(half, which, table_ref, first_row):
        other = 1 - which
        groups = list(range(ngroups))

        def start_groups(n):
            for grp in groups[:n]:
                for j in range(SUBLANES):
                    for k in range(TOP_K):
                        slot = table_ref[0, (first_row + grp * SUBLANES + j) * TOP_K + k]
                        pltpu.make_async_copy(ys_ref.at[pl.ds(slot, 1)], sets[other][k].at[grp, pl.ds(j, 1)],
                                              sems.at[other]).start(priority=k % 2)
            del groups[:n]

        rows = slice(half * tm, (half + 1) * tm)
        gates = gates_ref[rows, :]
        chunk = 2 * LANES
        per_piece = -(-ngroups // (d // chunk))
        parts = []
        for c in range(d // chunk):
            start_groups(per_piece)
            cs = slice(c * chunk, (c + 1) * chunk)
            routed = gates[:, 0:1] * sets[which][0][:, :, cs].reshape(tm, chunk)
            for k in range(1, TOP_K):
                routed = routed + gates[:, k:k + 1] * sets[which][k][:, :, cs].reshape(tm, chunk)
            parts.append(ALPHA * x_ref[rows, cs] + (routed + sh_ref[rows, cs]))
        start_groups(len(groups))
        y = _ln_rows(jnp.concatenate(parts, axis=1), g_ref[...], b_ref[...])
        if final:
            @pl.when(i < n_prompt_steps)
            def _():
                outs[0][rows, :] = y

            @pl.when(i >= n_prompt_steps)
            def _():
                outs[1][rows, :] = y
        else:
            outs[0][rows, :] = y
            outs[1][rows, :] = y.astype(BF16)

    @pl.when(i == 0)
    def _():
        gather(slot_ref, 0, 0)

    drain(0)
    reduce_tile(0, 0, slot_ref, tm)
    drain(1)
    reduce_tile(1, 1, slot_next_ref, 0)

    @pl.when(i == n_steps - 1)
    def _():
        drain(0)


def _combine(slot_t, gates, x1, shared, ys, g, b, t_prompt, final):
    t, d = x1.shape
    tm = _row_tile(math.gcd(t_prompt, t - t_prompt) // 2, 128)
    step = 2 * tm
    nps = t_prompt // step
    n_steps = t // step
    assert nps * step == t_prompt and n_steps * step == t
    vec = pl.BlockSpec((1, d), lambda i: (0, 0))
    rows = lambda w: pl.BlockSpec((step, w), lambda i: (i, 0))
    if final:
        out_specs = [pl.BlockSpec((step, d), lambda i: (jnp.minimum(i, nps - 1), 0)),
                     pl.BlockSpec((step, d), lambda i: (jnp.maximum(i - nps, 0), 0))]
        out_shape = [jax.ShapeDtypeStruct((t_prompt, d), F32), jax.ShapeDtypeStruct((t - t_prompt, d), F32)]
    else:
        out_specs = [rows(d), rows(d)]
        out_shape = [jax.ShapeDtypeStruct((t, d), F32), jax.ShapeDtypeStruct((t, d), BF16)]
    slots = _slot_tiles(slot_t, step)
    slot_spec = lambda shift: pl.BlockSpec((None, 1, step * TOP_K),
                                           lambda i: (jnp.minimum(i + shift, n_steps - 1), 0, 0),
                                           memory_space=pltpu.SMEM)
    return pl.pallas_call(
        functools.partial(_combine_kernel, n_prompt_steps=nps, n_steps=n_steps, final=final),
        grid=(n_steps,),
        in_specs=[slot_spec(0), slot_spec(1), rows(TOP_K), rows(d), rows(d), pl.BlockSpec(memory_space=pl.ANY),
                  vec, vec],
        out_specs=out_specs,
        out_shape=out_shape,
        scratch_shapes=[pltpu.VMEM((tm // SUBLANES, SUBLANES, d), F32)] * (2 * TOP_K) + [pltpu.SemaphoreType.DMA((2,))],
        compiler_params=_cparams(("arbitrary",), VMEM_LIMIT),
        name="combine",
    )(slots, slots, gates, x1, shared, ys, g, b)


def _filter_features(seq):
    t = np.linspace(0.0, 1.0, seq)[:, None]
    w = 2.0 * np.pi * np.arange(seq) / seq
    bands = np.linspace(1e-4, FILTER_BANDS - 1, FILTER_BANDS)
    ang = w[:, None] * bands[None, :]
    z = np.zeros((seq, FILTER_ORDER))
    z[:, :FILTER_EMB] = np.concatenate([t, np.cos(ang), -np.sin(ang)], axis=-1)
    return jnp.asarray(z, F32)


def _decay_rates():
    max_decay = math.log(DECAY_TARGET) / SHORT_DECAY_PCT
    min_decay = math.log(DECAY_TARGET) / LONG_DECAY_PCT
    return jnp.asarray(np.linspace(min_decay, max_decay, HYENA_WIDTH)[None, :], F32)


def kernel(x_prompt, x_sample, ln_in_g, ln_in_b, w_in, attn_sink, hy_conv_w, hy_conv_b, hy_f_w1, hy_f_b1,
           hy_f_freq, hy_f_w2, hy_f_b2, hy_f_w3, hy_skip, pool_w, pool_scale, out_norm_g, w_out, ln1_g, ln1_b,
           router_w, router_bias, exp_w_gate, exp_w_up, exp_w_down, sh_w_gate, sh_w_up, sh_w_down, ln2_g, ln2_b):
    bp, seq, d = x_prompt.shape
    bs = x_sample.shape[0]
    assert x_sample.shape[1] == seq and d == D_MODEL and seq % (FFT_N1 * SUBLANES) == 0
    nbatch = bp + bs
    t_prompt = bp * seq
    t = nbatch * seq
    n_blocks = t * TOP_K // MOE_MB + N_EXPERTS
    assert (t * TOP_K) % MOE_MB == 0

    plan = _FftPlan(seq)
    ctab, stab = _rope_tables(seq)
    w1p = jnp.pad(hy_f_w1, ((0, 0), (0, FILTER_ORDER - FILTER_EMB), (0, 0)))
    hspec = _hyena_filter_spectrum(plan, _filter_features(seq), w1p, hy_f_b1[:, None, :], hy_f_freq[:, None, :],
                                   hy_f_w2, hy_f_b2[:, None, :], hy_f_w3, _decay_rates())

    row = lambda v: v.reshape(1, -1)
    x, xb = _ln_in(x_prompt.reshape(t_prompt, d), x_sample.reshape(bs * seq, d), row(ln_in_g), row(ln_in_b))
    outs = None
    for l in range(DEPTH):
        proj3 = _in_proj(xb, w_in[l].astype(BF16)).reshape(nbatch, seq, IN_WIDTH)
        ya = _attention(proj3, attn_sink[l], ctab, stab)
        hsrc, hgate = _hyena_pre(proj3, hy_conv_w[l], row(hy_conv_b[l]))
        yh = _hyena(plan, hsrc, hgate, row(hy_skip[l]), hspec, l)
        yp = _pool(proj3, pool_w[l].astype(BF16), row(pool_scale[l]))
        x1, xpk, eidx_t, gate_t, rank_t, counts = _post_mixer(
            ya.reshape(t, ATTN_WIDTH), yh.reshape(t, HYENA_WIDTH), yp.reshape(t, POOL_WIDTH), x,
            row(out_norm_g[l]), w_out[l].astype(BF16), row(ln1_g[l]), row(ln1_b[l]), router_w[l], router_bias[l])
        slot_t, block_expert, n_used, last_block, has_rows = _dispatch_plan(eidx_t, rank_t, counts, n_blocks)
        xs, shared = _dispatch(xpk, slot_t, n_used, last_block, has_rows, sh_w_gate[l].astype(BF16),
                               sh_w_up[l].astype(BF16), sh_w_down[l].astype(BF16), n_blocks)
        ys = _experts(xs, block_expert, n_used, exp_w_gate, exp_w_up, exp_w_down, l, n_blocks)
        res = _combine(slot_t, gate_t.T, x1, shared, ys, row(ln2_g[l]), row(ln2_b[l]), t_prompt,
                       final=(l == DEPTH - 1))
        if l == DEPTH - 1:
            outs = res
        else:
            x, xb = res
    return outs[0].reshape(bp, seq, d), outs[1].reshape(bs, seq, d)
```

```python
import functools
import math

import jax
import jax.numpy as jnp
import numpy as np
from jax import lax
from jax.experimental import pallas as pl
from jax.experimental.pallas import tpu as pltpu

F32 = jnp.float32
BF16 = jnp.bfloat16
HIGHEST = lax.Precision.HIGHEST

D_MODEL = 2048
DEPTH = 2
HEAD_DIM = 128
ATTN_WIDTH = D_MODEL // 2
N_Q_HEADS = ATTN_WIDTH // HEAD_DIM
N_KV_HEADS = 2
Q_PER_KV = N_Q_HEADS // N_KV_HEADS
KV_WIDTH = N_KV_HEADS * HEAD_DIM
HYENA_WIDTH = D_MODEL // 4
POOL_WIDTH = D_MODEL - ATTN_WIDTH - HYENA_WIDTH
IN_WIDTH = ATTN_WIDTH + 2 * KV_WIDTH + 3 * HYENA_WIDTH + POOL_WIDTH
OFF_K = ATTN_WIDTH
OFF_V = OFF_K + KV_WIDTH
OFF_HY = OFF_V + KV_WIDTH
OFF_POOL = OFF_HY + 3 * HYENA_WIDTH

WINDOW = 128
BLOCK = 128
ROPE_THETA = 500000.0
ROPE_DIM = HEAD_DIM // 4
ROPE_HALF = ROPE_DIM // 2
NEG_BIG = -1e30
REMOVED = -3.0e38

SHORT_CONV = 3
FILTER_EMB = 33
FILTER_BANDS = (FILTER_EMB - 1) // 2
FILTER_ORDER = 64
DECAY_TARGET = 1e-2
SHORT_DECAY_PCT = 0.3
LONG_DECAY_PCT = 1.5

POOL_WINDOWS = (2, 4, 8, 16)
POOL_GROUP = POOL_WIDTH // len(POOL_WINDOWS)
OUT_GROUP = 128

N_EXPERTS = 64
TOP_K = 8
N_EXPERT_GROUPS = 8
EXPERTS_PER_GROUP = N_EXPERTS // N_EXPERT_GROUPS
TOPK_GROUPS = 4
EXPERT_HIDDEN = 512
SHARED_HIDDEN = 512
ROUTED_SCALE = 2.5

ALPHA = (2 * DEPTH) ** 0.25
LN_EPS = 1e-5
RMS_EPS = 1e-6

SUBLANES = 8
LANES = 128
VMEM_LIMIT = 56 * 1024 * 1024

FFT_N1 = 64
HY_CB = 128
MOE_MB = 512
HALF_D = D_MODEL // 2


def _cparams(sem, vmem=None):
    return pltpu.CompilerParams(dimension_semantics=sem, vmem_limit_bytes=vmem)


def _row_tile(t, pref):
    while t % pref:
        pref //= 2
    return pref


def _ln_rows(x, g, b):
    mu = jnp.mean(x, axis=-1, keepdims=True)
    xc = x - mu
    var = jnp.mean(xc * xc, axis=-1, keepdims=True)
    return xc * lax.rsqrt(var + LN_EPS) * g + b


def _pack_bf16_pair(x):
    h = x.shape[1] // 2
    hi = lax.bitcast_convert_type(x[:, :h].astype(BF16).astype(F32), jnp.uint32)
    lo = lax.bitcast_convert_type(x[:, h:].astype(BF16).astype(F32), jnp.uint32)
    return hi | (lo >> 16)


def _unpack_bf16_pair(pk):
    a = lax.bitcast_convert_type(pk & jnp.uint32(0xFFFF0000), F32).astype(BF16)
    b = lax.bitcast_convert_type(pk << 16, F32).astype(BF16)
    return a, b


def _ln_in_kernel(xp_ref, xs_ref, g_ref, b_ref, o_ref, ob_ref, *, n_prompt_blocks):
    i = pl.program_id(0)

    def emit(x):
        y = _ln_rows(x, g_ref[...], b_ref[...])
        o_ref[...] = y
        ob_ref[...] = y.astype(BF16)

    @pl.when(i < n_prompt_blocks)
    def _():
        emit(xp_ref[...])

    @pl.when(i >= n_prompt_blocks)
    def _():
        emit(xs_ref[...])


def _ln_in(xp, xs, g, b):
    tp, d = xp.shape
    ts = xs.shape[0]
    tm = _row_tile(math.gcd(tp, ts), 512)
    npb, nsb = tp // tm, ts // tm
    t = tp + ts
    return pl.pallas_call(
        functools.partial(_ln_in_kernel, n_prompt_blocks=npb),
        grid=(npb + nsb,),
        in_specs=[
            pl.BlockSpec((tm, d), lambda i: (jnp.minimum(i, npb - 1), 0)),
            pl.BlockSpec((tm, d), lambda i: (jnp.maximum(i - npb, 0), 0)),
            pl.BlockSpec((1, d), lambda i: (0, 0)),
            pl.BlockSpec((1, d), lambda i: (0, 0)),
        ],
        out_specs=[pl.BlockSpec((tm, d), lambda i: (i, 0)), pl.BlockSpec((tm, d), lambda i: (i, 0))],
        out_shape=[jax.ShapeDtypeStruct((t, d), F32), jax.ShapeDtypeStruct((t, d), BF16)],
        compiler_params=_cparams(("arbitrary",)),
        name="ln_in",
    )(xp, xs, g, b)


def _matmul_kernel(x_ref, w_ref, o_ref):
    o_ref[...] = jnp.dot(x_ref[...], w_ref[...], preferred_element_type=F32).astype(o_ref.dtype)


def _in_proj(xb, w):
    t, k = xb.shape
    n = w.shape[1]
    tm = _row_tile(t, 1024)
    tn = 512
    return pl.pallas_call(
        _matmul_kernel,
        grid=(t // tm, n // tn),
        in_specs=[pl.BlockSpec((tm, k), lambda i, j: (i, 0)), pl.BlockSpec((k, tn), lambda i, j: (0, j))],
        out_specs=pl.BlockSpec((tm, tn), lambda i, j: (i, j)),
        out_shape=jax.ShapeDtypeStruct((t, n), BF16),
        compiler_params=_cparams(("arbitrary", "arbitrary")),
        name="in_proj",
    )(xb, w)


def _rope_tables(seq):
    pos = np.arange(-BLOCK, seq + BLOCK, dtype=np.float64)
    inv = ROPE_THETA ** (-np.arange(ROPE_HALF, dtype=np.float64) / ROPE_HALF)
    ang = pos[:, None] * inv[None, :]
    c = np.ones((pos.shape[0], HEAD_DIM))
    s = np.zeros((pos.shape[0], HEAD_DIM))
    c[:, :ROPE_HALF] = np.cos(ang)
    c[:, ROPE_HALF:ROPE_DIM] = np.cos(ang)
    s[:, :ROPE_HALF] = -np.sin(ang)
    s[:, ROPE_HALF:ROPE_DIM] = np.sin(ang)
    return jnp.asarray(c, F32), jnp.asarray(s, F32)


def _rope(x, c, s):
    lane = lax.broadcasted_iota(jnp.int32, x.shape, 1)
    partner = jnp.where(lane < ROPE_HALF, pltpu.roll(x, HEAD_DIM - ROPE_HALF, 1), pltpu.roll(x, ROPE_HALF, 1))
    return x * c + partner * s


def _attn_kernel(sink_ref, q_ref, *refs, nb, qb):
    nkb = qb + 2
    k_refs, v_refs = refs[:nkb], refs[nkb:2 * nkb]
    ct_ref, st_ref, o_ref = refs[2 * nkb:]
    i = pl.program_id(1)
    base = pl.multiple_of(i * (qb * BLOCK), BLOCK)
    cw = ct_ref[pl.ds(base, nkb * BLOCK), :]
    sw = st_ref[pl.ds(base, nkb * BLOCK), :]
    kwin = jnp.concatenate([r[0] for r in k_refs], axis=0).astype(F32)
    vwin = jnp.concatenate([r[0] for r in v_refs], axis=0)
    q = q_ref[0].astype(F32)

    rows = Q_PER_KV * BLOCK
    rr = lax.broadcasted_iota(jnp.int32, (rows, 3 * BLOCK), 0) & (BLOCK - 1)
    cc = lax.broadcasted_iota(jnp.int32, (rows, 3 * BLOCK), 1)
    rel = cc - rr
    in_window = (rel >= BLOCK - WINDOW) & (rel <= BLOCK + WINDOW)
    rgrp = lax.broadcasted_iota(jnp.int32, (rows, 1), 0) // BLOCK

    for h in range(N_KV_HEADS):
        kh_all = _rope(kwin[:, h * HEAD_DIM:(h + 1) * HEAD_DIM], cw, sw).astype(BF16)
        vh_all = vwin[:, h * HEAD_DIM:(h + 1) * HEAD_DIM].astype(BF16)
        sink = jnp.zeros((rows, 1), F32)
        for g in range(Q_PER_KV):
            sink = jnp.where(rgrp == g, sink_ref[h * Q_PER_KV + g], sink)
        for sb in range(qb):
            blk = i * qb + sb
            lo = jnp.where(blk == 0, BLOCK, 0)
            hi = jnp.where(blk == nb - 1, 2 * BLOCK, 3 * BLOCK)
            valid = in_window & (cc >= lo) & (cc < hi)
            cq, sq = cw[(sb + 1) * BLOCK:(sb + 2) * BLOCK], sw[(sb + 1) * BLOCK:(sb + 2) * BLOCK]
            kh, vh = kh_all[sb * BLOCK:(sb + 3) * BLOCK], vh_all[sb * BLOCK:(sb + 3) * BLOCK]
            qs = []
            for g in range(Q_PER_KV):
                hq = h * Q_PER_KV + g
                qs.append(_rope(q[sb * BLOCK:(sb + 1) * BLOCK, hq * HEAD_DIM:(hq + 1) * HEAD_DIM], cq, sq).astype(BF16))
            qg = jnp.concatenate(qs, axis=0)
            s = lax.dot_general(qg, kh, (((1,), (1,)), ((), ())), preferred_element_type=F32) * (HEAD_DIM ** -0.5)
            s = jnp.where(valid, s, NEG_BIG)
            m = jnp.maximum(jnp.max(s, axis=-1, keepdims=True), sink)
            p = jnp.exp(s - m)
            denom = jnp.sum(p, axis=-1, keepdims=True) + jnp.exp(sink - m)
            o = jnp.dot((p / denom).astype(BF16), vh, preferred_element_type=F32)
            for g in range(Q_PER_KV):
                hq = h * Q_PER_KV + g
                o_ref[0, sb * BLOCK:(sb + 1) * BLOCK, hq * HEAD_DIM:(hq + 1) * HEAD_DIM] = o[g * BLOCK:(g + 1) * BLOCK]


def _attention(proj3, sink, ctab, stab):
    b, seq, _ = proj3.shape
    nb = seq // BLOCK
    qb = next(q for q in (4, 2, 1) if nb % q == 0)
    kcol, vcol = OFF_K // KV_WIDTH, OFF_V // KV_WIDTH

    def kv_spec(col, shift):
        return pl.BlockSpec((1, BLOCK, KV_WIDTH), lambda bi, i: (bi, jnp.clip(i * qb + shift, 0, nb - 1), col))

    shifts = range(-1, qb + 1)
    n_kv = 2 * len(shifts)
    return pl.pallas_call(
        functools.partial(_attn_kernel, nb=nb, qb=qb),
        grid=(b, nb // qb),
        in_specs=[pl.BlockSpec(memory_space=pltpu.SMEM),
                  pl.BlockSpec((1, qb * BLOCK, ATTN_WIDTH), lambda bi, i: (bi, i, 0))]
                 + [kv_spec(kcol, s) for s in shifts] + [kv_spec(vcol, s) for s in shifts]
                 + [pl.BlockSpec(ctab.shape, lambda bi, i: (0, 0)), pl.BlockSpec(stab.shape, lambda bi, i: (0, 0))],
        out_specs=pl.BlockSpec((1, qb * BLOCK, ATTN_WIDTH), lambda bi, i: (bi, i, 0)),
        out_shape=jax.ShapeDtypeStruct((b, seq, ATTN_WIDTH), F32),
        compiler_params=_cparams(("arbitrary", "arbitrary")),
        name="attention",
    )(sink, *([proj3] * (1 + n_kv)), ctab, stab)


class _FftPlan:
    def __init__(self, seq):
        self.seq = seq
        self.n = 2 * seq
        self.n1 = FFT_N1
        self.n2 = self.n // self.n1
        self.nh = seq // self.n1
        self.k2 = self.n2 // 2 + 1
        self.k2p = -(-self.k2 // SUBLANES) * SUBLANES
        self.srow = 2 * self.n1 + SUBLANES
        n1 = np.arange(self.n1)
        n2 = np.arange(self.nh)
        k2 = np.arange(self.k2)
        npos = self.n1 * n2[None, None, :] + n1[:, None, None]
        ang = 2.0 * np.pi * k2[None, :, None] * npos / self.n
        f1 = np.zeros((self.n1, 2 * self.k2p, self.nh))
        f1[:, :self.k2] = np.cos(ang)
        f1[:, self.k2p:self.k2p + self.k2] = -np.sin(ang)
        wk = np.full((self.k2,), 2.0)
        wk[0] = 1.0
        wk[-1] = 1.0
        gm = np.zeros((self.n1, self.nh, 2 * self.k2p))
        angt = np.transpose(ang, (0, 2, 1))
        gm[:, :, :self.k2] = np.cos(angt) * wk / self.n
        gm[:, :, self.k2p:self.k2p + self.k2] = -np.sin(angt) * wk / self.n
        a2 = 2.0 * np.pi * np.outer(n1, n1) / self.n1
        c2, s2 = np.cos(a2), np.sin(a2)
        self.f1 = _split3_lhs(f1)
        self.g = _split3_lhs(gm)
        self.m2 = _split3_lhs(np.block([[c2, s2], [-s2, c2]]))
        self.m2i = _split3_lhs(np.block([[c2, -s2], [s2, c2]]))


def _dotf(a, b):
    return jnp.dot(a, b, precision=HIGHEST, preferred_element_type=F32)


def _split3_lhs(m):
    m32 = np.asarray(m, np.float32)
    hi = m32.astype(BF16)
    lo = (m32 - hi.astype(np.float32)).astype(BF16)
    return jnp.asarray(np.concatenate([hi, hi, lo], axis=-1))


def _split3_rhs(x):
    hi = x.astype(BF16)
    lo = (x - hi.astype(F32)).astype(BF16)
    return jnp.concatenate([hi, lo, hi], axis=0)


def _dot3(lhs3, x):
    return jnp.dot(lhs3, _split3_rhs(x), preferred_element_type=F32)


def _lane_cat(parts):
    return parts[0] if len(parts) == 1 else jnp.concatenate(parts, axis=1)


def _fft_stage1(plan, load_rows, f1_ref, slab_refs):
    unroll = 4

    def body(grp, carry):
        for u in range(unroll):
            n1 = grp * unroll + u
            res = _dot3(f1_ref[n1], _lane_cat([ld(n1) for ld in load_rows]))
            for p, slab in enumerate(slab_refs):
                part = res[:, p * LANES:(p + 1) * LANES]
                slab[pl.ds(n1, plan.k2p, stride=plan.srow), :] = part[:plan.k2p]
                slab[pl.ds(plan.n1 + n1, plan.k2p, stride=plan.srow), :] = part[plan.k2p:]
        return carry

    lax.fori_loop(0, plan.n1 // unroll, body, 0)


def _slab_rows(plan, slab_refs, k2):
    off = pl.multiple_of(k2 * plan.srow, SUBLANES)
    return off, _lane_cat([s[pl.ds(off, 2 * plan.n1), :] for s in slab_refs])


def _hyena_filter_kernel(z_ref, w1_ref, b1_ref, fr_ref, w2_ref, b2_ref, w3f_ref, w3b_ref, dl_ref,
                         f1_ref, m2_ref, h_ref, srcf_ref, srcb_ref, slabf_ref, slabb_ref, hid_ref, *, plan):
    seq = plan.seq

    @pl.when(pl.program_id(1) == 0)
    def _():
        fr = fr_ref[0]
        h1 = jnp.sin(fr * (_dotf(z_ref[...], w1_ref[0]) + b1_ref[0]))
        hid_ref[...] = jnp.sin(fr * (_dotf(h1, w2_ref[0]) + b2_ref[0]))

    hid = hid_ref[...]
    row = lax.broadcasted_iota(jnp.int32, (seq, 1), 0)
    t = row.astype(F32) * (1.0 / (seq - 1))
    decay = jnp.exp(-t * jnp.abs(dl_ref[...]))
    srcf_ref[...] = _dotf(hid, w3f_ref[0]) * decay
    srcb_ref[...] = jnp.where(row == 0, 0.0, _dotf(hid, w3b_ref[0]) * decay)
    slabs = (slabf_ref, slabb_ref)
    _fft_stage1(plan, [lambda n1, r=r: r[pl.ds(n1, plan.nh, stride=plan.n1), :] for r in (srcf_ref, srcb_ref)],
                f1_ref, slabs)

    unroll = 5 if plan.k2 % 5 == 0 else 1

    def body(grp, carry):
        for u in range(unroll):
            k2 = grp * unroll + u
            _, a = _slab_rows(plan, slabs, k2)
            x = _dot3(m2_ref[...], a)
            h_ref[0, k2, :plan.n1, :] = x[:plan.n1, :LANES] + x[:plan.n1, LANES:]
            h_ref[0, k2, plan.n1:, :] = x[plan.n1:, :LANES] - x[plan.n1:, LANES:]
        return carry

    lax.fori_loop(0, plan.k2 // unroll, body, 0)


def _hyena_filter_spectrum(plan, z, w1p, b1, freq, w2, b2, w3, deltas):
    depth = w3.shape[0]
    cb = HY_CB
    ncb = HYENA_WIDTH // cb
    seq = plan.seq
    lay = lambda shp: pl.BlockSpec((1,) + shp, lambda l, j: (l,) + (0,) * len(shp))
    return pl.pallas_call(
        functools.partial(_hyena_filter_kernel, plan=plan),
        grid=(depth, ncb),
        in_specs=[
            pl.BlockSpec(z.shape, lambda l, j: (0, 0)),
            lay(w1p.shape[1:]), lay(b1.shape[1:]), lay(freq.shape[1:]), lay(w2.shape[1:]), lay(b2.shape[1:]),
            pl.BlockSpec((1, FILTER_ORDER, cb), lambda l, j: (l, 0, j)),
            pl.BlockSpec((1, FILTER_ORDER, cb), lambda l, j: (l, 0, ncb + j)),
            pl.BlockSpec((1, cb), lambda l, j: (0, j)),
            pl.BlockSpec(plan.f1.shape, lambda l, j: (0, 0, 0)),
            pl.BlockSpec(plan.m2.shape, lambda l, j: (0, 0)),
        ],
        out_specs=pl.BlockSpec((1, plan.k2, 2 * plan.n1, cb), lambda l, j: (l, 0, 0, j)),
        out_shape=jax.ShapeDtypeStruct((depth, plan.k2, 2 * plan.n1, HYENA_WIDTH), F32),
        scratch_shapes=[pltpu.VMEM((seq, cb), F32), pltpu.VMEM((seq, cb), F32),
                        pltpu.VMEM((plan.k2p * plan.srow, cb), F32), pltpu.VMEM((plan.k2p * plan.srow, cb), F32),
                        pltpu.VMEM((seq, FILTER_ORDER), F32)],
        compiler_params=_cparams(("arbitrary", "arbitrary"), VMEM_LIMIT),
        name="hyena_filter",
    )(z, w1p, b1, freq, w2, b2, w3, w3, deltas, plan.f1, plan.m2)


def _short_conv(u, w, b, seq):
    row = lax.broadcasted_iota(jnp.int32, (seq, 1), 0)
    prev = jnp.where(row >= 1, pltpu.roll(u, 1, 0), 0.0)
    nxt = jnp.where(row < seq - 1, pltpu.roll(u, seq - 1, 0), 0.0)
    return prev * w[0:1] + u * w[1:2] + nxt * w[2:3] + b


def _hyena_pre_kernel(x0_ref, x1_ref, v_ref, w0_ref, w1_ref, wv_ref, b0_ref, b1_ref, bv_ref, src_ref, gate_ref, *, seq):
    gate_ref[0] = _short_conv(x0_ref[0].astype(F32), w0_ref[...], b0_ref[...], seq)
    src_ref[0] = (_short_conv(v_ref[0].astype(F32), wv_ref[...], bv_ref[...], seq)
                  * _short_conv(x1_ref[0].astype(F32), w1_ref[...], b1_ref[...], seq))


def _hyena_pre(proj3, conv_w, conv_b):
    b, seq, _ = proj3.shape
    cb = HY_CB
    ncb = HYENA_WIDTH // cb
    c0 = OFF_HY // cb

    def u_spec(part):
        return pl.BlockSpec((1, seq, cb), lambda bi, j: (bi, 0, c0 + part * ncb + j))

    def w_spec(part):
        return pl.BlockSpec((SHORT_CONV, cb), lambda bi, j: (0, part * ncb + j))

    def b_spec(part):
        return pl.BlockSpec((1, cb), lambda bi, j: (0, part * ncb + j))

    out = pl.BlockSpec((1, seq, cb), lambda bi, j: (bi, 0, j))
    shp = jax.ShapeDtypeStruct((b, seq, HYENA_WIDTH), F32)
    return pl.pallas_call(
        functools.partial(_hyena_pre_kernel, seq=seq),
        grid=(b, ncb),
        in_specs=[u_spec(0), u_spec(1), u_spec(2), w_spec(0), w_spec(1), w_spec(2), b_spec(0), b_spec(1), b_spec(2)],
        out_specs=[out, out],
        out_shape=[shp, shp],
        compiler_params=_cparams(("arbitrary", "arbitrary"), VMEM_LIMIT),
        name="hyena_pre",
    )(proj3, proj3, proj3, conv_w, conv_w, conv_w, conv_b, conv_b, conv_b)


def _hyena_kernel(src_ref, gate_ref, skip_ref, h_ref, f1_ref, m2_ref, m2i_ref, g_ref, o_ref, *slab_refs, plan):
    npar = len(slab_refs)
    n1c = plan.n1
    _fft_stage1(plan, [lambda n1, p=p: src_ref[p, pl.ds(n1, plan.nh, stride=n1c), :] for p in range(npar)],
                f1_ref, slab_refs)

    unroll2 = 5 if plan.k2 % 5 == 0 else 1

    def stage2(grp, carry):
        for u in range(unroll2):
            k2 = grp * unroll2 + u
            off, a = _slab_rows(plan, slab_refs, k2)
            x = _dot3(m2_ref[...], a)
            h = _lane_cat([h_ref[0, k2]] * npar)
            xr, xi = x[:n1c], x[n1c:]
            hr, hi = h[:n1c], h[n1c:]
            y = jnp.concatenate([xr * hr - xi * hi, xr * hi + xi * hr], axis=0)
            back = _dot3(m2i_ref[...], y)
            for p, slab in enumerate(slab_refs):
                slab[pl.ds(off, 2 * n1c), :] = back[:, p * LANES:(p + 1) * LANES]
        return carry

    lax.fori_loop(0, plan.k2 // unroll2, stage2, 0)

    unroll3 = 4

    def stage3(grp, carry):
        for u in range(unroll3):
            n1 = grp * unroll3 + u
            a = _lane_cat([jnp.concatenate([s[pl.ds(n1, plan.k2p, stride=plan.srow), :],
                                            s[pl.ds(n1c + n1, plan.k2p, stride=plan.srow), :]], axis=0)
                           for s in slab_refs])
            y = _dot3(g_ref[n1], a)
            for p in range(npar):
                o_ref[p, pl.ds(n1, plan.nh, stride=n1c), :] = y[:, p * LANES:(p + 1) * LANES]
        return carry

    lax.fori_loop(0, n1c // unroll3, stage3, 0)
    for p in range(npar):
        o_ref[p] = (o_ref[p] + src_ref[p] * skip_ref[...]) * gate_ref[p]


def _hyena(plan, src, gate, skip, hspec, layer):
    b, seq, _ = src.shape
    cb = HY_CB
    ncb = HYENA_WIDTH // cb
    npar = 2 if b % 2 == 0 else 1
    once = pl.Buffered(1)
    const = lambda a: pl.BlockSpec(a.shape, lambda j, bi: (0,) * a.ndim, pipeline_mode=once)
    seq_spec = pl.BlockSpec((npar, seq, cb), lambda j, bi: (bi, 0, j))
    return pl.pallas_call(
        functools.partial(_hyena_kernel, plan=plan),
        grid=(ncb, b // npar),
        in_specs=[
            seq_spec, seq_spec,
            pl.BlockSpec((1, cb), lambda j, bi: (0, j)),
            pl.BlockSpec((1, plan.k2, 2 * plan.n1, cb), lambda j, bi: (layer, 0, 0, j), pipeline_mode=once),
            const(plan.f1), const(plan.m2), const(plan.m2i), const(plan.g),
        ],
        out_specs=seq_spec,
        out_shape=jax.ShapeDtypeStruct((b, seq, HYENA_WIDTH), F32),
        scratch_shapes=[pltpu.VMEM((plan.k2p * plan.srow, cb), F32)] * npar,
        compiler_params=_cparams(("arbitrary", "arbitrary"), VMEM_LIMIT),
        name="hyena",
    )(src, gate, skip, hspec, plan.f1, plan.m2, plan.m2i, plan.g)


def _pool_kernel(u_ref, w_ref, sc_ref, o_ref, *, seq):
    row = lax.broadcasted_iota(jnp.int32, (seq, 1), 0)

    def back(x, s):
        return jnp.where(row >= s, pltpu.roll(x, s, 0), 0.0)

    def fwd(x, s):
        return jnp.where(row < seq - s, pltpu.roll(x, seq - s, 0), 0.0)

    for gi, win in enumerate(POOL_WINDOWS):
        half = win // 2
        u = u_ref[0, :, gi * POOL_GROUP:(gi + 1) * POOL_GROUP].astype(F32)
        ahead, behind, s = u, u, 1
        while s < half:
            ahead = ahead + fwd(ahead, s)
            behind = behind + back(behind, s)
            s *= 2
        total = ahead + back(behind, 1)
        cnt = (jnp.minimum(row + half, seq) - jnp.maximum(row - half, 0)).astype(F32)
        diff = total / cnt - u
        y = jnp.dot(diff.astype(BF16), w_ref[gi], preferred_element_type=F32)
        o_ref[0, :, gi * POOL_GROUP:(gi + 1) * POOL_GROUP] = y * sc_ref[:, gi * POOL_GROUP:(gi + 1) * POOL_GROUP]


def _pool(proj3, pool_w, pool_scale):
    b, seq, _ = proj3.shape
    return pl.pallas_call(
        functools.partial(_pool_kernel, seq=seq),
        grid=(b,),
        in_specs=[
            pl.BlockSpec((1, seq, POOL_WIDTH), lambda bi: (bi, 0, OFF_POOL // POOL_WIDTH)),
            pl.BlockSpec(pool_w.shape, lambda bi: (0, 0, 0)),
            pl.BlockSpec((1, POOL_WIDTH), lambda bi: (0, 0)),
        ],
        out_specs=pl.BlockSpec((1, seq, POOL_WIDTH), lambda bi: (bi, 0, 0)),
        out_shape=jax.ShapeDtypeStruct((b, seq, POOL_WIDTH), F32),
        compiler_params=_cparams(("arbitrary",), VMEM_LIMIT),
        name="pool",
    )(proj3, pool_w, pool_scale)


def _post_mixer_kernel(ya_ref, yh_ref, yp_ref, x_ref, gn_ref, w_ref, g_ref, b_ref, rw_ref, bias_ref, tri_ref,
                       o_ref, opk_ref, eidx_ref, gate_ref, rank_ref, cnt_ref, carry_ref):
    parts = []
    for src in (ya_ref, yh_ref, yp_ref):
        for j in range(src.shape[1] // OUT_GROUP):
            c = src[:, j * OUT_GROUP:(j + 1) * OUT_GROUP]
            parts.append(c * lax.rsqrt(jnp.mean(c * c, axis=-1, keepdims=True) + RMS_EPS))
    yn = (jnp.concatenate(parts, axis=-1) * gn_ref[...]).astype(BF16)
    mix = jnp.dot(yn, w_ref[...], preferred_element_type=F32)
    x1 = _ln_rows(ALPHA * x_ref[...] + mix, g_ref[...], b_ref[...])
    o_ref[...] = x1
    opk_ref[...] = _pack_bf16_pair(x1)
    _route(x1, rw_ref, bias_ref, tri_ref, eidx_ref, gate_ref, rank_ref, cnt_ref, carry_ref)


def _post_mixer(ya, yh, yp, x, gn, w_out, g, b, router_w, router_bias):
    t, d = x.shape
    tm = _row_tile(t, 512)
    row = lambda w: pl.BlockSpec((tm, w), lambda i: (i, 0))
    vec = pl.BlockSpec((1, d), lambda i: (0, 0))
    once = pl.Buffered(1)
    const = lambda a: pl.BlockSpec(a.shape, lambda i: (0,) * a.ndim, pipeline_mode=once)
    rw3 = _split3_lhs_traced(router_w.T)
    bias_b = jnp.broadcast_to(router_bias[:, None], (N_EXPERTS, tm)).astype(F32)
    tri = jnp.asarray(np.triu(np.ones((tm, tm)), 1), BF16)
    kt = lambda dt: jax.ShapeDtypeStruct((TOP_K, t), dt)
    kspec = pl.BlockSpec((TOP_K, tm), lambda i: (0, i))
    return pl.pallas_call(
        _post_mixer_kernel,
        grid=(t // tm,),
        in_specs=[row(ATTN_WIDTH), row(HYENA_WIDTH), row(POOL_WIDTH), row(d), vec, const(w_out), vec, vec,
                  const(rw3), const(bias_b), const(tri)],
        out_specs=[row(d), row(d // 2), kspec, kspec, kspec, pl.BlockSpec((N_EXPERTS, LANES), lambda i: (0, 0))],
        out_shape=[jax.ShapeDtypeStruct((t, d), F32), jax.ShapeDtypeStruct((t, d // 2), jnp.uint32),
                   kt(jnp.int32), kt(F32), kt(jnp.int32), jax.ShapeDtypeStruct((N_EXPERTS, LANES), F32)],
        scratch_shapes=[pltpu.VMEM((N_EXPERTS, LANES), F32)],
        compiler_params=_cparams(("arbitrary",), VMEM_LIMIT),
        name="post_mixer",
    )(ya, yh, yp, x, gn, w_out, g, b, rw3, bias_b, tri)


def _split3_lhs_traced(m):
    hi = m.astype(BF16)
    lo = (m - hi.astype(F32)).astype(BF16)
    return jnp.concatenate([hi, hi, lo], axis=-1)


def _route(x1, rw_ref, bias_ref, tri_ref, eidx_ref, gate_ref, rank_ref, cnt_ref, carry_ref):
    tm = x1.shape[0]

    @pl.when(pl.program_id(0) == 0)
    def _():
        carry_ref[...] = jnp.zeros_like(carry_ref)

    xh = x1.astype(BF16)
    xl = (x1 - xh.astype(F32)).astype(BF16)
    logits = lax.dot_general(rw_ref[...], jnp.concatenate([xh, xl, xh], axis=1), (((1,), (1,)), ((), ())),
                             preferred_element_type=F32)
    scores = jax.nn.sigmoid(logits)
    choice = scores + bias_ref[...]
    sub = lax.broadcasted_iota(jnp.int32, (EXPERTS_PER_GROUP, tm), 0)
    far = jnp.int32(N_EXPERTS)

    def first_argmax(v):
        m = jnp.max(v, axis=0, keepdims=True)
        return m, jnp.min(jnp.where(v == m, sub, far), axis=0, keepdims=True)

    tiles, stiles, gscore = [], [], []
    for g in range(N_EXPERT_GROUPS):
        c = choice[g * EXPERTS_PER_GROUP:(g + 1) * EXPERTS_PER_GROUP]
        tiles.append(c)
        stiles.append(scores[g * EXPERTS_PER_GROUP:(g + 1) * EXPERTS_PER_GROUP])
        m1, i1 = first_argmax(c)
        m2 = jnp.max(jnp.where(sub == i1, REMOVED, c), axis=0, keepdims=True)
        gscore.append(m1 + m2)
    cur = jnp.concatenate(gscore, axis=0)
    gsel = jnp.zeros(cur.shape, F32)
    for _ in range(TOPK_GROUPS):
        _, ig = first_argmax(cur)
        hit = sub == ig
        gsel = jnp.where(hit, 1.0, gsel)
        cur = jnp.where(hit, REMOVED, cur)

    eids = [sub + g * EXPERTS_PER_GROUP for g in range(N_EXPERT_GROUPS)]
    masked = [jnp.where(gsel[g:g + 1] > 0.5, tiles[g], NEG_BIG) for g in range(N_EXPERT_GROUPS)]
    picked = [jnp.zeros((EXPERTS_PER_GROUP, tm), F32) for _ in range(N_EXPERT_GROUPS)]
    idxs, sels = [], []
    for _ in range(TOP_K):
        m = functools.reduce(jnp.maximum, [jnp.max(v, axis=0, keepdims=True) for v in masked])
        idx = functools.reduce(jnp.minimum, [jnp.min(jnp.where(v == m, e, far), axis=0, keepdims=True)
                                             for v, e in zip(masked, eids)])
        sc = jnp.zeros((1, tm), F32)
        for g in range(N_EXPERT_GROUPS):
            hit = eids[g] == idx
            sc = sc + jnp.sum(jnp.where(hit, stiles[g], 0.0), axis=0, keepdims=True)
            masked[g] = jnp.where(hit, REMOVED, masked[g])
            picked[g] = jnp.where(hit, 1.0, picked[g])
        idxs.append(idx)
        sels.append(sc)
    total = functools.reduce(lambda a, c: a + c, sels)
    for k in range(TOP_K):
        eidx_ref[k:k + 1, :] = idxs[k]
        gate_ref[k:k + 1, :] = sels[k] / total * ROUTED_SCALE

    sel = jnp.concatenate(picked, axis=0)
    before = jnp.dot(sel.astype(BF16), tri_ref[...], preferred_element_type=F32) + carry_ref[:, 0:1]
    for k in range(TOP_K):
        r = jnp.zeros((1, tm), F32)
        for g in range(N_EXPERT_GROUPS):
            r = r + jnp.sum(jnp.where(eids[g] == idxs[k], before[g * EXPERTS_PER_GROUP:(g + 1) * EXPERTS_PER_GROUP], 0.0),
                            axis=0, keepdims=True)
        rank_ref[k:k + 1, :] = r.astype(jnp.int32)
    carry_ref[...] = carry_ref[...] + jnp.sum(sel, axis=1, keepdims=True)
    cnt_ref[...] = carry_ref[...]


def _dispatch_plan(eidx_t, rank_t, counts, n_blocks):
    mb = MOE_MB
    cnt = counts[:, 0].astype(jnp.int32)
    padded = (cnt + mb - 1) // mb * mb
    pend = jnp.cumsum(padded)
    pstart = pend - padded
    experts = jnp.arange(N_EXPERTS, dtype=jnp.int32)
    slot_t = rank_t + jnp.sum(jnp.where(eidx_t[:, :, None] == experts, pstart, 0), axis=-1)
    block_end = pend // mb
    blocks = jnp.arange(n_blocks, dtype=jnp.int32)
    block_expert = jnp.minimum(jnp.sum((block_end[None, :] <= blocks[:, None]).astype(jnp.int32), axis=1),
                               N_EXPERTS - 1).astype(jnp.int32)
    n_used = block_end[-1:].astype(jnp.int32)
    last_block = (block_end - 1).astype(jnp.int32)
    has_rows = (padded > 0).astype(jnp.int32)
    return slot_t.astype(jnp.int32), block_expert, n_used, last_block, has_rows


def _dispatch_kernel(last_ref, has_ref, nused_ref, slot_ref, x_ref, wg_ref, wu_ref, wd_ref, xs_ref, sh_ref,
                     zero_ref, zsem, sem, *, n_blocks):
    mb = MOE_MB
    tm = x_ref.shape[0]

    def zero_copy(block):
        return pltpu.make_async_copy(zero_ref, xs_ref.at[pl.ds(pl.multiple_of(block * mb, mb), mb)], zsem)

    @pl.when(pl.program_id(0) == 0)
    def _():
        zero_ref[...] = jnp.zeros_like(zero_ref)

        def start_e(e, c):
            @pl.when(has_ref[e] > 0)
            def _():
                zero_copy(last_ref[e]).start()
            return c

        def wait_e(e, c):
            @pl.when(has_ref[e] > 0)
            def _():
                zero_copy(last_ref[e]).wait()
            return c

        def start_b(blk, c):
            zero_copy(blk).start()
            return c

        def wait_b(blk, c):
            zero_copy(blk).wait()
            return c

        lax.fori_loop(0, N_EXPERTS, start_e, 0)
        lax.fori_loop(nused_ref[0], n_blocks, start_b, 0)
        lax.fori_loop(0, N_EXPERTS, wait_e, 0)
        lax.fori_loop(nused_ref[0], n_blocks, wait_b, 0)

    groups = list(range(tm // SUBLANES))

    def scatter_rows(n_groups):
        for grp in groups[:n_groups]:
            for j in range(SUBLANES):
                r = grp * SUBLANES + j
                for k in range(TOP_K):
                    slot = slot_ref[0, r * TOP_K + k]
                    pltpu.make_async_copy(x_ref.at[pl.ds(r, 1)], xs_ref.at[pl.ds(slot, 1)], sem).start(priority=k % 2)
        del groups[:n_groups]

    chunk = 2 * LANES
    n_pieces = 2 * (SHARED_HIDDEN // chunk) + D_MODEL // chunk
    per_piece = -(-len(groups) // n_pieces)
    a, b = _unpack_bf16_pair(x_ref[...])
    pre = []
    for w_ref in (wg_ref, wu_ref):
        cols = []
        for c in range(SHARED_HIDDEN // chunk):
            scatter_rows(per_piece)
            cs = slice(c * chunk, (c + 1) * chunk)
            cols.append(jnp.dot(a, w_ref[:HALF_D, cs], preferred_element_type=F32)
                        + jnp.dot(b, w_ref[HALF_D:, cs], preferred_element_type=F32))
        pre.append(jnp.concatenate(cols, axis=1))
    hid = (pre[0] * jax.nn.sigmoid(pre[0]) * pre[1]).astype(BF16)
    for c in range(D_MODEL // chunk):
        scatter_rows(per_piece)
        cs = slice(c * chunk, (c + 1) * chunk)
        sh_ref[:, cs] = jnp.dot(hid, wd_ref[:, cs], preferred_element_type=F32)
    scatter_rows(len(groups))
    for k in range(TOP_K):
        pltpu.make_async_copy(x_ref, xs_ref.at[pl.ds(0, tm)], sem).wait()


def _slot_tiles(slot_t, tm):
    t = slot_t.shape[1]
    return slot_t.T.reshape(t // tm, 1, tm * TOP_K)


def _dispatch(xpk, slot_t, n_used, last_block, has_rows, wg, wu, wd, n_blocks):
    t, hw = xpk.shape
    tm = _row_tile(t, 256)
    full = lambda a: pl.BlockSpec(a.shape, lambda i, *_: (0,) * a.ndim)
    grid_spec = pltpu.PrefetchScalarGridSpec(
        num_scalar_prefetch=3,
        grid=(t // tm,),
        in_specs=[
            pl.BlockSpec((None, 1, tm * TOP_K), lambda i, *_: (i, 0, 0), memory_space=pltpu.SMEM),
            pl.BlockSpec((tm, hw), lambda i, *_: (i, 0)),
            full(wg), full(wu), full(wd),
        ],
        out_specs=[pl.BlockSpec(memory_space=pl.ANY), pl.BlockSpec((tm, D_MODEL), lambda i, *_: (i, 0))],
        scratch_shapes=[pltpu.VMEM((MOE_MB, hw), jnp.uint32), pltpu.SemaphoreType.DMA, pltpu.SemaphoreType.DMA],
    )
    return pl.pallas_call(
        functools.partial(_dispatch_kernel, n_blocks=n_blocks),
        grid_spec=grid_spec,
        out_shape=[jax.ShapeDtypeStruct((n_blocks * MOE_MB, hw), jnp.uint32), jax.ShapeDtypeStruct((t, D_MODEL), F32)],
        compiler_params=_cparams(("arbitrary",), VMEM_LIMIT),
        name="dispatch",
    )(last_block, has_rows, n_used, _slot_tiles(slot_t, tm), xpk, wg, wu, wd)


def _swiglu_packed(pk, wg, wu, wd):
    a, b = _unpack_bf16_pair(pk)
    gate = jnp.dot(a, wg[:HALF_D], preferred_element_type=F32) + jnp.dot(b, wg[HALF_D:], preferred_element_type=F32)
    up = jnp.dot(a, wu[:HALF_D], preferred_element_type=F32) + jnp.dot(b, wu[HALF_D:], preferred_element_type=F32)
    hid = (gate * jax.nn.sigmoid(gate) * up).astype(BF16)
    return jnp.dot(hid, wd, preferred_element_type=F32)


def _experts_kernel(be_ref, nused_ref, nxt_ref, xs_ref, wg_hbm, wu_hbm, wd_hbm, y_ref,
                    sg_ref, su_ref, sd_ref, wgb_ref, wub_ref, wdb_ref, sems, grp_ref, *, layer):
    i = pl.program_id(0)
    e = be_ref[i]
    first = (i == 0) | (e != be_ref[jnp.maximum(i - 1, 0)])

    def fetch(expert, slot):
        pairs = ((wg_hbm, sg_ref), (wu_hbm, su_ref), (wd_hbm, sd_ref))
        return [pltpu.make_async_copy(w.at[layer, expert], s.at[slot], sems.at[slot, j])
                for j, (w, s) in enumerate(pairs)]

    @pl.when(i == 0)
    def _():
        grp_ref[0] = 0
        for c in fetch(e, 0):
            c.start()

    @pl.when(first)
    def _():
        slot = grp_ref[0] & 1
        for c in fetch(e, slot):
            c.wait()
        nxt = nxt_ref[i]

        @pl.when(nxt != e)
        def _():
            for c in fetch(nxt, 1 - slot):
                c.start()

        wgb_ref[...] = sg_ref[slot].astype(BF16)
        wub_ref[...] = su_ref[slot].astype(BF16)
        wdb_ref[...] = sd_ref[slot].astype(BF16)
        grp_ref[0] = grp_ref[0] + 1

    @pl.when(i < nused_ref[0])
    def _():
        y_ref[...] = _pack_bf16_pair(_swiglu_packed(xs_ref[...], wgb_ref, wub_ref, wdb_ref[...]))

    @pl.when(i >= nused_ref[0])
    def _():
        y_ref[...] = jnp.zeros_like(y_ref)


def _next_expert(block_expert):
    experts = jnp.arange(N_EXPERTS, dtype=jnp.int32)
    present = jnp.any(block_expert[:, None] == experts[None, :], axis=0)
    later = jnp.where((experts[None, :] > block_expert[:, None]) & present[None, :], experts[None, :], N_EXPERTS)
    nxt = jnp.min(later, axis=1)
    return jnp.where(nxt == N_EXPERTS, block_expert, nxt).astype(jnp.int32)


def _experts(xs, block_expert, n_used, wg, wu, wd, layer, n_blocks):
    hw = xs.shape[1]
    d = wd.shape[3]
    hbm = pl.BlockSpec(memory_space=pl.ANY)
    grid_spec = pltpu.PrefetchScalarGridSpec(
        num_scalar_prefetch=3,
        grid=(n_blocks,),
        in_specs=[pl.BlockSpec((MOE_MB, hw), lambda i, *_: (i, 0)), hbm, hbm, hbm],
        out_specs=pl.BlockSpec((MOE_MB, d // 2), lambda i, *_: (i, 0)),
        scratch_shapes=[pltpu.VMEM((2,) + wg.shape[2:], F32), pltpu.VMEM((2,) + wu.shape[2:], F32),
                        pltpu.VMEM((2,) + wd.shape[2:], F32),
                        pltpu.VMEM(wg.shape[2:], BF16), pltpu.VMEM(wu.shape[2:], BF16), pltpu.VMEM(wd.shape[2:], BF16),
                        pltpu.SemaphoreType.DMA((2, 3)), pltpu.SMEM((1,), jnp.int32)],
    )
    return pl.pallas_call(
        functools.partial(_experts_kernel, layer=layer),
        grid_spec=grid_spec,
        out_shape=jax.ShapeDtypeStruct((n_blocks * MOE_MB, d // 2), jnp.uint32),
        compiler_params=_cparams(("arbitrary",), VMEM_LIMIT),
        name="experts",
    )(block_expert, n_used, _next_expert(block_expert), xs, wg, wu, wd)


def _combine_kernel(slot_ref, slot_next_ref, gates_ref, x_ref, sh_ref, ys_ref, g_ref, b_ref, *rest,
                    n_prompt_steps, n_steps, final):
    outs = rest[:2]
    sets = (rest[2:2 + TOP_K], rest[2 + TOP_K:2 + 2 * TOP_K])
    sems = rest[2 + 2 * TOP_K]
    tm = x_ref.shape[0] // 2
    d = x_ref.shape[1]
    ngroups = tm // SUBLANES
    i = pl.program_id(0)

    def gather(table_ref, first_row, which):
        def body(grp, c):
            for j in range(SUBLANES):
                for k in range(TOP_K):
                    slot = table_ref[0, (first_row + grp * SUBLANES + j) * TOP_K + k]
                    pltpu.make_async_copy(ys_ref.at[pl.ds(slot, 1)], sets[which][k].at[grp, pl.ds(j, 1)],
                                          sems.at[which]).start(priority=k % 2)
            return c

        lax.fori_loop(0, ngroups, body, 0)

    def drain(which):
        def body(grp, c):
            for k in range(TOP_K):
                pltpu.make_async_copy(ys_ref.at[pl.ds(0, SUBLANES)], sets[which][k].at[grp], sems.at[which]).wait()
            return c

        lax.fori_loop(0, ngroups, body, 0)

    def reduce_tile(half, which, table_ref, first_row):
        other = 1 - which
        groups = list(range(ngroups))

        def start_groups(n):
            for grp in groups[:n]:
                for j in range(SUBLANES):
                    for k in range(TOP_K):
                        slot = table_ref[0, (first_row + grp * SUBLANES + j) * TOP_K + k]
                        pltpu.make_async_copy(ys_ref.at[pl.ds(slot, 1)], sets[other][k].at[grp, pl.ds(j, 1)],
                                              sems.at[other]).start(priority=k % 2)
            del groups[:n]

        rows = slice(half * tm, (half + 1) * tm)
        gates = gates_ref[rows, :]
        chunk = 2 * LANES
        hw = d // 2
        per_piece = -(-ngroups // (hw // chunk))
        hi_parts, lo_parts = [], []
        for c in range(hw // chunk):
            start_groups(per_piece)
            cs = slice(c * chunk, (c + 1) * chunk)
            cs_lo = slice(hw + c * chunk, hw + (c + 1) * chunk)
            r_hi = r_lo = 0.0
            for k in range(TOP_K):
                pk = sets[which][k][:, :, cs].reshape(tm, chunk)
                gk = gates[:, k:k + 1]
                r_hi = r_hi + gk * lax.bitcast_convert_type(pk & jnp.uint32(0xFFFF0000), F32)
                r_lo = r_lo + gk * lax.bitcast_convert_type(pk << 16, F32)
            hi_parts.append(ALPHA * x_ref[rows, cs] + (r_hi + sh_ref[rows, cs]))
            lo_parts.append(ALPHA * x_ref[rows, cs_lo] + (r_lo + sh_ref[rows, cs_lo]))
        start_groups(len(groups))
        y = _ln_rows(jnp.concatenate(hi_parts + lo_parts, axis=1), g_ref[...], b_ref[...])
        if final:
            @pl.when(i < n_prompt_steps)
            def _():
                outs[0][rows, :] = y

            @pl.when(i >= n_prompt_steps)
            def _():
                outs[1][rows, :] = y
        else:
            outs[0][rows, :] = y
            outs[1][rows, :] = y.astype(BF16)

    @pl.when(i == 0)
    def _():
        gather(slot_ref, 0, 0)

    drain(0)
    reduce_tile(0, 0, slot_ref, tm)
    drain(1)
    reduce_tile(1, 1, slot_next_ref, 0)

    @pl.when(i == n_steps - 1)
    def _():
        drain(0)


def _combine(slot_t, gates, x1, shared, ys, g, b, t_prompt, final):
    t, d = x1.shape
    tm = _row_tile(math.gcd(t_prompt, t - t_prompt) // 2, 128)
    step = 2 * tm
    nps = t_prompt // step
    n_steps = t // step
    assert nps * step == t_prompt and n_steps * step == t
    vec = pl.BlockSpec((1, d), lambda i: (0, 0))
    rows = lambda w: pl.BlockSpec((step, w), lambda i: (i, 0))
    if final:
        out_specs = [pl.BlockSpec((step, d), lambda i: (jnp.minimum(i, nps - 1), 0)),
                     pl.BlockSpec((step, d), lambda i: (jnp.maximum(i - nps, 0), 0))]
        out_shape = [jax.ShapeDtypeStruct((t_prompt, d), F32), jax.ShapeDtypeStruct((t - t_prompt, d), F32)]
    else:
        out_specs = [rows(d), rows(d)]
        out_shape = [jax.ShapeDtypeStruct((t, d), F32), jax.ShapeDtypeStruct((t, d), BF16)]
    slots = _slot_tiles(slot_t, step)
    slot_spec = lambda shift: pl.BlockSpec((None, 1, step * TOP_K),
                                           lambda i: (jnp.minimum(i + shift, n_steps - 1), 0, 0),
                                           memory_space=pltpu.SMEM)
    return pl.pallas_call(
        functools.partial(_combine_kernel, n_prompt_steps=nps, n_steps=n_steps, final=final),
        grid=(n_steps,),
        in_specs=[slot_spec(0), slot_spec(1), rows(TOP_K), rows(d), rows(d), pl.BlockSpec(memory_space=pl.ANY),
                  vec, vec],
        out_specs=out_specs,
        out_shape=out_shape,
        scratch_shapes=[pltpu.VMEM((tm // SUBLANES, SUBLANES, d // 2), jnp.uint32)] * (2 * TOP_K)
                       + [pltpu.SemaphoreType.DMA((2,))],
        compiler_params=_cparams(("arbitrary",), VMEM_LIMIT),
        name="combine",
    )(slots, slots, gates, x1, shared, ys, g, b)


def _filter_features(seq):
    t = np.linspace(0.0, 1.0, seq)[:, None]
    w = 2.0 * np.pi * np.arange(seq) / seq
    bands = np.linspace(1e-4, FILTER_BANDS - 1, FILTER_BANDS)
    ang = w[:, None] * bands[None, :]
    z = np.zeros((seq, FILTER_ORDER))
    z[:, :FILTER_EMB] = np.concatenate([t, np.cos(ang), -np.sin(ang)], axis=-1)
    return jnp.asarray(z, F32)


def _decay_rates():
    max_decay = math.log(DECAY_TARGET) / SHORT_DECAY_PCT
    min_decay = math.log(DECAY_TARGET) / LONG_DECAY_PCT
    return jnp.asarray(np.linspace(min_decay, max_decay, HYENA_WIDTH)[None, :], F32)


def kernel(x_prompt, x_sample, ln_in_g, ln_in_b, w_in, attn_sink, hy_conv_w, hy_conv_b, hy_f_w1, hy_f_b1,
           hy_f_freq, hy_f_w2, hy_f_b2, hy_f_w3, hy_skip, pool_w, pool_scale, out_norm_g, w_out, ln1_g, ln1_b,
           router_w, router_bias, exp_w_gate, exp_w_up, exp_w_down, sh_w_gate, sh_w_up, sh_w_down, ln2_g, ln2_b):
    bp, seq, d = x_prompt.shape
    bs = x_sample.shape[0]
    assert x_sample.shape[1] == seq and d == D_MODEL and seq % (FFT_N1 * SUBLANES) == 0
    nbatch = bp + bs
    t_prompt = bp * seq
    t = nbatch * seq
    n_blocks = t * TOP_K // MOE_MB + N_EXPERTS
    assert (t * TOP_K) % MOE_MB == 0

    plan = _FftPlan(seq)
    ctab, stab = _rope_tables(seq)
    w1p = jnp.pad(hy_f_w1, ((0, 0), (0, FILTER_ORDER - FILTER_EMB), (0, 0)))
    hspec = _hyena_filter_spectrum(plan, _filter_features(seq), w1p, hy_f_b1[:, None, :], hy_f_freq[:, None, :],
                                   hy_f_w2, hy_f_b2[:, None, :], hy_f_w3, _decay_rates())

    row = lambda v: v.reshape(1, -1)
    x, xb = _ln_in(x_prompt.reshape(t_prompt, d), x_sample.reshape(bs * seq, d), row(ln_in_g), row(ln_in_b))
    outs = None
    for l in range(DEPTH):
        proj3 = _in_proj(xb, w_in[l].astype(BF16)).reshape(nbatch, seq, IN_WIDTH)
        ya = _attention(proj3, attn_sink[l], ctab, stab)
        hsrc, hgate = _hyena_pre(proj3, hy_conv_w[l], row(hy_conv_b[l]))
        yh = _hyena(plan, hsrc, hgate, row(hy_skip[l]), hspec, l)
        yp = _pool(proj3, pool_w[l].astype(BF16), row(pool_scale[l]))
        x1, xpk, eidx_t, gate_t, rank_t, counts = _post_mixer(
            ya.reshape(t, ATTN_WIDTH), yh.reshape(t, HYENA_WIDTH), yp.reshape(t, POOL_WIDTH), x,
            row(out_norm_g[l]), w_out[l].astype(BF16), row(ln1_g[l]), row(ln1_b[l]), router_w[l], router_bias[l])
        slot_t, block_expert, n_used, last_block, has_rows = _dispatch_plan(eidx_t, rank_t, counts, n_blocks)
        xs, shared = _dispatch(xpk, slot_t, n_used, last_block, has_rows, sh_w_gate[l].astype(BF16),
                               sh_w_up[l].astype(BF16), sh_w_down[l].astype(BF16), n_blocks)
        ys = _experts(xs, block_expert, n_used, exp_w_gate, exp_w_up, exp_w_down, l, n_blocks)
        res = _combine(slot_t, gate_t.T, x1, shared, ys, row(ln2_g[l]), row(ln2_b[l]), t_prompt,
                       final=(l == DEPTH - 1))
        if l == DEPTH - 1:
            outs = res
        else:
            x, xb = res
    return outs[0].reshape(bp, seq, d), outs[1].reshape(bs, seq, d)
```
